```python
import math
import jax, jax.numpy as jnp
from jax import lax
import numpy as np

D_MODEL = 1024
BATCH = 32
SEQ = 256
DEPTH = 4
DEC_BATCH = 2
DEC_SEQ = 4096
PAST_LEN = 256

GRID_W = 64
N_RET_HEADS = 4
RET_DK = 128
RET_DV = 128
RET_W = N_RET_HEADS * RET_DV
RET_CHUNK = 128
CONV_W = 512
CONV_K = 3
N_NA_HEADS = 8
NA_HEAD_DIM = 64
NA_W = N_NA_HEADS * NA_HEAD_DIM
NA_KH = 8
NA_KW = 16
BRANCH_W = 512
N_BRANCH = 3
N_SPLIT = 10
MIX_IN = N_SPLIT * BRANCH_W
D_FF = 2816
N_MOD = 9
ROPE_BASE = 10000.0
EPS = 1e-6
NEG_INF = -1e30

kernel_name = 'hybrid_diffusion_retention_conv_natten_step'


def rmsnorm(x, g):
    x32 = x.astype(jnp.float32)
    y = x32 * lax.rsqrt(jnp.mean(x32 * x32, axis=-1, keepdims=True) + EPS)
    return (y * g.astype(jnp.float32)).astype(x.dtype)


def split_heads(a, n_heads):
    b, l, _ = a.shape
    return a.reshape(b, l, n_heads, -1).transpose(0, 2, 1, 3)


def merge_heads(a):
    b, h, l, d = a.shape
    return a.transpose(0, 2, 1, 3).reshape(b, l, h * d)


def swiglu(u, w1, w2):
    a, b = jnp.split(u @ w1, 2, axis=-1)
    return (jax.nn.silu(a) * b) @ w2


def axial_rope(x):
    l = x.shape[2]
    t = jnp.arange(l)
    half = x.shape[-1] // 2
    quarter = half // 2
    inv_freq = ROPE_BASE ** (-jnp.arange(quarter, dtype=jnp.float32) * 2.0 / half)

    def rotate(xa, pos):
        ang = pos.astype(jnp.float32)[:, None] * inv_freq[None, :]
        cos, sin = jnp.cos(ang), jnp.sin(ang)
        x1 = xa[..., :quarter].astype(jnp.float32)
        x2 = xa[..., quarter:].astype(jnp.float32)
        return jnp.concatenate([x1 * cos - x2 * sin, x1 * sin + x2 * cos], axis=-1)

    out = jnp.concatenate([rotate(x[..., :half], t // GRID_W),
                           rotate(x[..., half:], t % GRID_W)], axis=-1)
    return out.astype(x.dtype)


def retention_scan(q, k, v, log_g, s0):
    b, h, l, _ = q.shape
    c = RET_CHUNK
    n = l // c
    idx = jnp.arange(c, dtype=jnp.float32)
    diff = idx[:, None] - idx[None, :]
    decay_in = jnp.where(diff >= 0, jnp.exp(log_g[:, None, None] * jnp.maximum(diff, 0.0)), 0.0)
    xi = jnp.exp(log_g[:, None] * (idx + 1.0))[None, :, :, None]
    zeta = jnp.exp(log_g[:, None] * (c - 1.0 - idx))[None, :, :, None]
    g_chunk = jnp.exp(log_g * c)[None, :, None, None]

    def to_chunks(a):
        return jnp.moveaxis(a.reshape(b, h, n, c, a.shape[-1]), 2, 0)

    def step(s, inp):
        qi, ki, vi = inp
        inner = jnp.einsum('bhqd,bhkd->bhqk', qi, ki) * decay_in[None]
        o = (jnp.einsum('bhqk,bhkv->bhqv', inner, vi)
             + jnp.einsum('bhqd,bhdv->bhqv', qi, s) * xi)
        s_new = s * g_chunk + jnp.einsum('bhkd,bhkv->bhdv', ki * zeta, vi)
        return s_new, o

    s_fin, o = lax.scan(step, s0.astype(jnp.float32), (to_chunks(q), to_chunks(k), to_chunks(v)))
    o = jnp.moveaxis(o, 0, 2).reshape(b, h, l, -1)
    return o, s_fin


def head_norm(o):
    mu = jnp.mean(o, axis=-1, keepdims=True)
    var = jnp.mean(jnp.square(o - mu), axis=-1, keepdims=True)
    return (o - mu) * lax.rsqrt(var + EPS)


def short_conv(b_gate, c_gate, h, w, bias):
    z = c_gate * h
    zp = jnp.pad(z, ((0, 0), (1, 1), (0, 0)))
    y = zp[:, :-2] * w[0] + zp[:, 1:-1] * w[1] + zp[:, 2:] * w[2] + bias
    return b_gate * y


def context_attention(q, k, v):
    s = jnp.einsum('bhqd,bhkd->bhqk', q, k).astype(jnp.float32) * (NA_HEAD_DIM ** -0.5)
    p = jax.nn.softmax(s, axis=-1).astype(v.dtype)
    return jnp.einsum('bhqk,bhkd->bhqd', p, v)


def neighbourhood_attention(q, k, v, k_ctx, v_ctx, rpb):
    b, h, l, hd = q.shape
    rows = l // GRID_W
    kh = min(NA_KH, rows)
    r = jnp.arange(rows)
    r0 = jnp.clip(r - NA_KH // 2, 0, rows - kh)
    row_idx = r0[:, None] + jnp.arange(kh)[None, :]
    cq = jnp.arange(GRID_W)
    c0 = jnp.clip(cq - NA_KW // 2, 0, GRID_W - NA_KW)
    col_ok = (cq[None, :] >= c0[:, None]) & (cq[None, :] < c0[:, None] + NA_KW)
    dr = row_idx - r[:, None] + (NA_KH - 1)
    dc = jnp.clip(cq[None, :] - cq[:, None] + (NA_KW - 1), 0, 2 * NA_KW - 2)
    bias = rpb.astype(jnp.float32)[:, dr[:, :, None, None], dc[None, None, :, :]]
    bias = jnp.where(col_ok[None, None, None], bias, NEG_INF)
    bias = bias.transpose(0, 1, 3, 2, 4).reshape(h, rows, GRID_W, kh * GRID_W)
    qg = q.reshape(b, h, rows, GRID_W, hd)
    kg = k.reshape(b, h, rows, GRID_W, hd)[:, :, row_idx].reshape(b, h, rows, kh * GRID_W, hd)
    vg = v.reshape(b, h, rows, GRID_W, hd)[:, :, row_idx].reshape(b, h, rows, kh * GRID_W, hd)
    scale = hd ** -0.5
    s_loc = jnp.einsum('bhrqd,bhrkd->bhrqk', qg, kg).astype(jnp.float32) * scale + bias[None]
    s_ctx = jnp.einsum('bhrqd,bhkd->bhrqk', qg, k_ctx).astype(jnp.float32) * scale
    p = jax.nn.softmax(jnp.concatenate([s_loc, s_ctx], axis=-1), axis=-1).astype(v.dtype)
    n_loc = kh * GRID_W
    o = (jnp.einsum('bhrqk,bhrkd->bhrqd', p[..., :n_loc], vg)
         + jnp.einsum('bhrqk,bhkd->bhrqd', p[..., n_loc:], v_ctx))
    return o.reshape(b, h, l, hd)


def token_mixer(u, l, W, ctx):
    b, n, _ = u.shape
    proj = u @ W['w_in'][l]
    q_r, k_r, v_r, g_r, c_b, c_c, c_h, q_n, k_n, v_n = jnp.split(proj, N_SPLIT, axis=-1)
    q_r = split_heads(q_r, N_RET_HEADS) * (RET_DK ** -0.5)
    k_r = split_heads(k_r, N_RET_HEADS)
    v_r = split_heads(v_r, N_RET_HEADS)
    log_g = jax.nn.log_sigmoid(W['ret_decay_logit'][l].astype(jnp.float32))
    if ctx is None:
        s_f0 = jnp.zeros((b, N_RET_HEADS, RET_DK, RET_DV), jnp.float32)
        s_b0 = s_f0
    else:
        s_f0, s_b0, k_ctx, v_ctx = ctx
        q_r = axial_rope(q_r)
        k_r = axial_rope(k_r)
    o_f, s_f = retention_scan(q_r, k_r, v_r, log_g[0], s_f0)
    o_b, s_b = retention_scan(jnp.flip(q_r, 2), jnp.flip(k_r, 2), jnp.flip(v_r, 2), log_g[1], s_b0)
    o_ret = head_norm(o_f + jnp.flip(o_b, 2))
    ret_out = merge_heads(o_ret).astype(u.dtype) * jax.nn.silu(g_r)
    conv_out = short_conv(c_b, c_c, c_h, W['conv_w'][l], W['conv_b'][l])
    q_n = split_heads(q_n, N_NA_HEADS)
    k_n = split_heads(k_n, N_NA_HEADS)
    v_n = split_heads(v_n, N_NA_HEADS)
    if ctx is None:
        o_na = context_attention(q_n, k_n, v_n)
        ctx_tensors = (jnp.stack([s_f, s_b], axis=1), k_n, v_n)
    else:
        o_na = neighbourhood_attention(q_n, k_n, v_n, k_ctx, v_ctx, W['na_rpb'][l])
        ctx_tensors = None
    na_out = merge_heads(o_na)
    branches = jnp.stack([ret_out, conv_out, na_out], axis=2)
    widened = jnp.einsum('blnc,ncd->blnd', branches, W['w_branch'][l])
    gates = jax.nn.sigmoid(u @ W['w_merge'][l] + W['b_merge'][l]).reshape(b, n, N_BRANCH, D_MODEL)
    merged = jnp.sum(gates * widened, axis=2)
    return merged @ W['w_out'][l], ctx_tensors


def trunk_layer(x, cond, l, W, ctx):
    bm = cond.shape[0]
    mod = (jax.nn.silu(cond) @ W['w_mod'][l] + W['b_mod'][l]).reshape(bm, N_MOD, D_MODEL)
    g = W['norm_g'][l]
    h = rmsnorm(x, g[0]) * (1.0 + mod[:, None, 1]) + mod[:, None, 0]
    x = x + 0.5 * mod[:, None, 2] * swiglu(h, W['ffn_w1'][l, 0], W['ffn_w2'][l, 0])
    u = rmsnorm(x, g[1]) * (1.0 + mod[:, None, 4]) + mod[:, None, 3]
    mix, ctx_tensors = token_mixer(u, l, W, ctx)
    x = x + mod[:, None, 5] * mix
    h = rmsnorm(x, g[2]) * (1.0 + mod[:, None, 7]) + mod[:, None, 6]
    x = x + 0.5 * mod[:, None, 8] * swiglu(h, W['ffn_w1'][l, 1], W['ffn_w2'][l, 1])
    return x, ctx_tensors


def setup_inputs(seed: int = 0) -> dict:
    key = jax.random.key(seed)
    ks = jax.random.split(key, 24)
    f32 = jnp.float32

    def nrm(k, shape, s):
        return jax.random.normal(k, shape, f32) * s

    a = 5.0 + jnp.arange(N_RET_HEADS, dtype=f32)
    decay0 = jnp.log(2.0 ** a - 1.0)
    return {
        'x_prompt': nrm(ks[0], (BATCH, SEQ, D_MODEL), 1.0),
        'x_sample': nrm(ks[1], (DEC_BATCH, DEC_SEQ, D_MODEL), 1.0),
        'c': nrm(ks[2], (DEC_BATCH, D_MODEL), 1.0),
        'state_ret': nrm(ks[3], (DEC_BATCH, DEPTH, 2, N_RET_HEADS, RET_DK, RET_DV), 0.5),
        'cache_na_k': nrm(ks[4], (DEC_BATCH, DEPTH, N_NA_HEADS, PAST_LEN, NA_HEAD_DIM), 1.0),
        'cache_na_v': nrm(ks[5], (DEC_BATCH, DEPTH, N_NA_HEADS, PAST_LEN, NA_HEAD_DIM), 1.0),
        'c_ctx': nrm(ks[6], (D_MODEL,), 1.0),
        'norm_g': 1.0 + nrm(ks[7], (DEPTH, 3, D_MODEL), 0.02),
        'w_mod': nrm(ks[8], (DEPTH, D_MODEL, N_MOD * D_MODEL), 0.5 * D_MODEL ** -0.5),
        'b_mod': nrm(ks[9], (DEPTH, N_MOD * D_MODEL), 0.01),
        'ffn_w1': nrm(ks[10], (DEPTH, 2, D_MODEL, 2 * D_FF), D_MODEL ** -0.5),
        'ffn_w2': nrm(ks[11], (DEPTH, 2, D_FF, D_MODEL), D_FF ** -0.5),
        'w_in': nrm(ks[12], (DEPTH, D_MODEL, MIX_IN), D_MODEL ** -0.5),
        'ret_decay_logit': decay0 + nrm(ks[13], (DEPTH, 2, N_RET_HEADS), 0.1),
        'conv_w': nrm(ks[14], (DEPTH, CONV_K, CONV_W), CONV_K ** -0.5),
        'conv_b': nrm(ks[15], (DEPTH, CONV_W), 0.01),
        'na_rpb': nrm(ks[16], (DEPTH, N_NA_HEADS, 2 * NA_KH - 1, 2 * NA_KW - 1), 0.1),
        'w_branch': nrm(ks[17], (DEPTH, N_BRANCH, BRANCH_W, D_MODEL), BRANCH_W ** -0.5),
        'w_merge': nrm(ks[18], (DEPTH, D_MODEL, N_BRANCH * D_MODEL), D_MODEL ** -0.5),
        'b_merge': nrm(ks[19], (DEPTH, N_BRANCH * D_MODEL), 0.01),
        'w_out': nrm(ks[20], (DEPTH, D_MODEL, D_MODEL), D_MODEL ** -0.5),
        'final_g': 1.0 + nrm(ks[21], (D_MODEL,), 0.02),
    }


def reference(x_prompt, x_sample, c, state_ret, cache_na_k, cache_na_v, c_ctx,
              norm_g, w_mod, b_mod, ffn_w1, ffn_w2, w_in, ret_decay_logit, conv_w, conv_b,
              na_rpb, w_branch, w_merge, b_merge, w_out, final_g):
    W = {'norm_g': norm_g, 'w_mod': w_mod, 'b_mod': b_mod, 'ffn_w1': ffn_w1, 'ffn_w2': ffn_w2,
         'w_in': w_in, 'ret_decay_logit': ret_decay_logit, 'conv_w': conv_w, 'conv_b': conv_b,
         'na_rpb': na_rpb, 'w_branch': w_branch, 'w_merge': w_merge, 'b_merge': b_merge,
         'w_out': w_out}
    h = x_prompt
    cond_ctx = c_ctx[None, :]
    s_list, k_list, v_list = [], [], []
    for l in range(DEPTH):
        h, (s_l, k_l, v_l) = trunk_layer(h, cond_ctx, l, W, None)
        s_list.append(s_l)
        k_list.append(k_l)
        v_list.append(v_l)
    y_prompt = rmsnorm(h, final_g)
    new_state_ret = jnp.stack(s_list, axis=1)
    new_cache_na_k = jnp.stack(k_list, axis=1)
    new_cache_na_v = jnp.stack(v_list, axis=1)
    z = x_sample
    for l in range(DEPTH):
        ctx = (state_ret[:, l, 0], state_ret[:, l, 1], cache_na_k[:, l], cache_na_v[:, l])
        z, _ = trunk_layer(z, c, l, W, ctx)
    y_sample = rmsnorm(z, final_g)
    return (y_prompt, y_sample, new_state_ret, new_cache_na_k, new_cache_na_v)
```

```python
import functools

import jax
import jax.numpy as jnp
from jax import lax
from jax.experimental import pallas as pl
from jax.experimental.pallas import tpu as pltpu

D_MODEL = 1024
GRID_W = 64
N_RET_HEADS = 4
RET_DK = 128
RET_DV = 128
RET_CHUNK = 128
CONV_K = 3
N_NA_HEADS = 8
NA_HEAD_DIM = 64
NA_KH = 8
NA_KW = 16
BRANCH_W = 512
N_BRANCH = 3
N_SPLIT = 10
D_FF = 2816
N_MOD = 9
ROPE_BASE = 10000.0
EPS = 1e-6
NEG_INF = -1e30

BF16 = jnp.bfloat16
F32 = jnp.float32

VMEM_LIMIT_BYTES = 56 * 1024 * 1024
COND_ROWS = 8
TOKEN_BLOCK = 512
FF_CHUNK = 256
NA_ROWS = 4
NA_WIN = NA_ROWS + NA_KH - 1
HALO = 8


def _params(n_axes):
    return pltpu.CompilerParams(dimension_semantics=("arbitrary",) * n_axes,
                                vmem_limit_bytes=VMEM_LIMIT_BYTES)


def _resident(block_shape, index_map):
    return pl.BlockSpec(block_shape, index_map, pipeline_mode=pl.Buffered(1))


def _dot(a, b):
    return jnp.dot(a, b, preferred_element_type=F32)


def _dot_nt(a, b):
    return lax.dot_general(a, b, (((1,), (1,)), ((), ())), preferred_element_type=F32)


def _dot_tn(a, b):
    return lax.dot_general(a, b, (((0,), (0,)), ((), ())), preferred_element_type=F32)


def _norm_mod(x, g_row, scale_row, shift_row):
    ms = jnp.mean(x * x, axis=-1, keepdims=True)
    y = x * lax.rsqrt(ms + EPS)
    return (y * g_row) * (1.0 + scale_row) + shift_row


def _mod_kernel(cond_ref, w_ref, b_ref, o_ref):
    a = jax.nn.silu(cond_ref[...]).astype(BF16)
    o_ref[...] = _dot(a, w_ref[...].astype(BF16)) + b_ref[...]


def _mod_call(cond, w_mod, b_mod):
    depth, d, n = w_mod.shape
    tn = 1024
    return pl.pallas_call(
        _mod_kernel,
        grid=(depth, n // tn),
        in_specs=[pl.BlockSpec((COND_ROWS, d), lambda l, j: (0, 0)),
                  pl.BlockSpec((None, d, tn), lambda l, j: (l, 0, j)),
                  pl.BlockSpec((None, 1, tn), lambda l, j: (l, 0, j))],
        out_specs=pl.BlockSpec((None, COND_ROWS, tn), lambda l, j: (l, 0, j)),
        out_shape=jax.ShapeDtypeStruct((depth, COND_ROWS, n), F32),
        compiler_params=_params(2),
        name="mod_vectors",
    )(cond, w_mod, b_mod.reshape(depth, 1, n))


def _ffn_kernel(x_ref, mod_ref, g_ref, w1_ref, w2_ref, o_ref, hb_ref, acc_ref, *, gi, mi, nj):
    x = x_ref[...]
    h = _norm_mod(x, g_ref[gi:gi + 1, :], mod_ref[mi + 1:mi + 2, :], mod_ref[mi:mi + 1, :])
    hb_ref[...] = h.astype(BF16)
    acc_ref[...] = jnp.zeros_like(acc_ref)

    def body(j, carry):
        hb = hb_ref[...]
        a = _dot(hb, w1_ref[0, j])
        b = _dot(hb, w1_ref[1, j])
        t = (jax.nn.silu(a) * b).astype(BF16)
        acc_ref[...] += _dot(t, w2_ref[j])
        return carry

    lax.fori_loop(0, nj, body, 0)
    o_ref[...] = x + (0.5 * mod_ref[mi + 2:mi + 3, :]) * acc_ref[...]


def _mixin_kernel(x_ref, mod_ref, g_ref, w_ref, o_ref):
    u = _norm_mod(x_ref[...], g_ref[1:2, :], mod_ref[4:5, :], mod_ref[3:4, :]).astype(BF16)
    for k in range(N_SPLIT):
        sl = slice(k * BRANCH_W, (k + 1) * BRANCH_W)
        o_ref[:, sl] = _dot(u, w_ref[:, sl])


def _ret_chunk(qi, ki, vi, s, dm, xi, zeta, gc):
    qb = qi.astype(BF16)
    vb = vi.astype(BF16)
    inner = _dot_nt(qb, ki.astype(BF16)) * dm
    o = _dot(inner.astype(BF16), vb) + _dot(qb, s.astype(BF16)) * xi
    s_new = s * gc + _dot_tn((ki * zeta).astype(BF16), vb)
    return o, s_new


def _head_norm(o):
    mu = jnp.mean(o, axis=-1, keepdims=True)
    var = jnp.mean(jnp.square(o - mu), axis=-1, keepdims=True)
    return (o - mu) * lax.rsqrt(var + EPS)


def _ret_ctx_kernel(q_ref, k_ref, v_ref, g_ref, dec_ref, o_ref, st_ref, *, n_chunks):
    c = RET_CHUNK
    outs = []
    for h in range(N_RET_HEADS):
        sl = slice(h * RET_DK, (h + 1) * RET_DK)
        q = q_ref[:, sl] * (RET_DK ** -0.5)
        k = k_ref[:, sl]
        v = v_ref[:, sl]
        o_tot = [None] * n_chunks
        for d in range(2):
            s = jnp.zeros((RET_DK, RET_DV), F32)
            order = range(n_chunks) if d == 0 else range(n_chunks - 1, -1, -1)
            for ci in order:
                rows = slice(ci * c, (ci + 1) * c)
                o, s = _ret_chunk(q[rows], k[rows], v[rows], s, dec_ref[d, h, 0], dec_ref[d, h, 1],
                                  dec_ref[d, h, 2], dec_ref[d, h, 3])
                o_tot[ci] = o if o_tot[ci] is None else o_tot[ci] + o
            st_ref[d, h] = s
        o = jnp.concatenate(o_tot, axis=0)
        outs.append(_head_norm(o) * jax.nn.silu(g_ref[:, sl]))
    o_ref[...] = jnp.concatenate(outs, axis=-1)


def _ret_lat_kernel(q_ref, k_ref, v_ref, g_ref, cos_ref, sin_ref, s0_ref, dec_ref, o_ref, s_scr, of_scr, *,
                    n_chunks):
    p = pl.program_id(1)
    ci = pl.program_id(2)

    @pl.when(ci == 0)
    def _():
        s_scr[...] = s0_ref[...]

    cos = cos_ref[...]
    sin = sin_ref[...]
    lane = lax.broadcasted_iota(jnp.int32, (RET_CHUNK, RET_DK), 1)
    first = (lane % (RET_DK // 2)) < (RET_DK // 4)

    def rope(x):
        swapped = jnp.where(first, pltpu.roll(x, RET_DK - RET_DK // 4, 1), pltpu.roll(x, RET_DK // 4, 1))
        return x * cos + swapped * sin

    chunk = jnp.where(p == 0, ci, n_chunks - 1 - ci)
    row0 = pl.multiple_of(chunk * RET_CHUNK, RET_CHUNK)
    for h in range(N_RET_HEADS):
        sl = slice(h * RET_DK, (h + 1) * RET_DK)
        q = rope(q_ref[:, sl] * (RET_DK ** -0.5))
        k = rope(k_ref[:, sl])
        o, s_new = _ret_chunk(q, k, v_ref[:, sl], s_scr[h], dec_ref[h, 0], dec_ref[h, 1], dec_ref[h, 2],
                              dec_ref[h, 3])
        s_scr[h] = s_new

        @pl.when(p == 0)
        def _():
            of_scr[pl.ds(row0, RET_CHUNK), sl] = o

        @pl.when(p == 1)
        def _():
            tot = of_scr[pl.ds(row0, RET_CHUNK), sl] + o
            o_ref[:, sl] = _head_norm(tot) * jax.nn.silu(g_ref[:, sl])


def _softmax_parts(parts):
    m = functools.reduce(jnp.maximum, [jnp.max(s, axis=-1, keepdims=True) for s in parts])
    es = [jnp.exp(s - m) for s in parts]
    den = functools.reduce(lambda a, b: a + b, [jnp.sum(e, axis=-1, keepdims=True) for e in es])
    return [e / den for e in es]


def _ctx_attn_kernel(q_ref, k_ref, v_ref, o_ref, ck_ref, cv_ref):
    outs = []
    for h in range(N_NA_HEADS):
        sl = slice(h * NA_HEAD_DIM, (h + 1) * NA_HEAD_DIM)
        kh = k_ref[:, sl]
        vh = v_ref[:, sl]
        ck_ref[h] = kh
        cv_ref[h] = vh
        s = _dot_nt(q_ref[:, sl].astype(BF16), kh.astype(BF16)) * (NA_HEAD_DIM ** -0.5)
        (p,) = _softmax_parts([s])
        outs.append(_dot(p.astype(BF16), vh.astype(BF16)))
    o_ref[...] = jnp.concatenate(outs, axis=-1)


def _na_kernel(q_ref, k_ref, v_ref, kc_ref, vc_ref, bias_ref, o_ref, *, n_rows):
    i = pl.program_id(2)
    win0 = jnp.clip(NA_ROWS * i - NA_KH // 2, 0, n_rows - NA_WIN)
    start = pl.multiple_of(win0 * GRID_W, GRID_W)
    scale = NA_HEAD_DIM ** -0.5
    outs = []
    for hh in range(2):
        sl = slice(hh * NA_HEAD_DIM, (hh + 1) * NA_HEAD_DIM)
        qh = q_ref[:, sl].astype(BF16)
        kw = k_ref[pl.ds(start, NA_WIN * GRID_W), sl].astype(BF16)
        vw = v_ref[pl.ds(start, NA_WIN * GRID_W), sl].astype(BF16)
        s_loc = _dot_nt(qh, kw) * scale + bias_ref[hh]
        s_ctx = _dot_nt(qh, kc_ref[hh].astype(BF16)) * scale
        p_loc, p_ctx = _softmax_parts([s_loc, s_ctx])
        outs.append(_dot(p_loc.astype(BF16), vw) + _dot(p_ctx.astype(BF16), vc_ref[hh].astype(BF16)))
    o_ref[...] = jnp.concatenate(outs, axis=-1)


def _mixout_kernel(x_ref, mod_ref, g_ref, cb_ref, cc_ref, ch_ref, ccp_ref, chp_ref, ccn_ref, chn_ref,
                   retc_ref, retl_ref, nac_ref, nal_ref, cw_ref, cbias_ref, wm_ref, bm_ref, wb_ref, wo_ref,
                   o_ref, *, tm, n_ctx, seq, dec_seq):
    i = pl.program_id(0)
    is_ctx = i * tm < n_ctx
    x = x_ref[...]
    ub = _norm_mod(x, g_ref[1:2, :], mod_ref[4:5, :], mod_ref[3:4, :]).astype(BF16)

    z = cc_ref[...] * ch_ref[...]
    z_before = (ccp_ref[...] * chp_ref[...])[HALO - 1:HALO, :]
    z_after = (ccn_ref[...] * chn_ref[...])[0:1, :]
    row = lax.broadcasted_iota(jnp.int32, z.shape, 0)
    last_pos = jnp.where(is_ctx, seq - 1, dec_seq - 1)
    pos = (i * tm - jnp.where(is_ctx, 0, n_ctx) + row) & last_pos
    z_prev = jnp.where(row == 0, z_before, pltpu.roll(z, 1, 0))
    z_prev = jnp.where(pos == 0, 0.0, z_prev)
    z_next = jnp.where(row == tm - 1, z_after, pltpu.roll(z, tm - 1, 0))
    z_next = jnp.where(pos == last_pos, 0.0, z_next)
    y = z_prev * cw_ref[0:1, :] + z * cw_ref[1:2, :] + z_next * cw_ref[2:3, :] + cbias_ref[...]
    conv_out = cb_ref[...] * y

    ret = jnp.where(is_ctx, retc_ref[...], retl_ref[...])
    na = jnp.where(is_ctx, nac_ref[...], nal_ref[...])
    merged = None
    for b, br in enumerate((ret, conv_out, na)):
        sl = slice(b * D_MODEL, (b + 1) * D_MODEL)
        gate = jax.nn.sigmoid(_dot(ub, wm_ref[:, sl]) + bm_ref[:, sl])
        term = gate * _dot(br.astype(BF16), wb_ref[b])
        merged = term if merged is None else merged + term
    mix = _dot(merged.astype(BF16), wo_ref[...])
    o_ref[...] = x + mod_ref[5:6, :] * mix


def _final_kernel(x_ref, g_ref, o_ref):
    x = x_ref[...]
    ms = jnp.mean(x * x, axis=-1, keepdims=True)
    o_ref[...] = (x * lax.rsqrt(ms + EPS)) * g_ref[...]


def _rope_tables(length, dim):
    t = jnp.arange(length)
    half = dim // 2
    quarter = half // 2
    inv_freq = ROPE_BASE ** (-jnp.arange(quarter, dtype=F32) * 2.0 / half)

    def tables(pos):
        ang = pos.astype(F32)[:, None] * inv_freq[None, :]
        cos, sin = jnp.cos(ang), jnp.sin(ang)
        return jnp.concatenate([cos, cos], axis=-1), jnp.concatenate([-sin, sin], axis=-1)

    cr, sr = tables(t // GRID_W)
    cc, sc = tables(t % GRID_W)
    return jnp.concatenate([cr, cc], axis=-1), jnp.concatenate([sr, sc], axis=-1)


def _decay_tables(decay_logit):
    c = RET_CHUNK
    log_g = jax.nn.log_sigmoid(decay_logit.astype(F32))
    idx = jnp.arange(c, dtype=F32)
    diff = idx[:, None] - idx[None, :]
    out = []
    for d in range(2):
        lg = log_g[d]
        decay_in = jnp.where(diff >= 0, jnp.exp(lg[:, None, None] * jnp.maximum(diff, 0.0)), 0.0)
        xi = jnp.exp(lg[:, None] * (idx + 1.0))
        zeta = jnp.exp(lg[:, None] * (c - 1.0 - idx))
        g_chunk = jnp.exp(lg * c)
        if d == 1:
            decay_in = jnp.swapaxes(decay_in, 1, 2)
            xi = xi[:, ::-1]
            zeta = zeta[:, ::-1]
        full = (c, c)
        out.append(jnp.stack([decay_in,
                              jnp.broadcast_to(xi[:, :, None], (N_RET_HEADS,) + full),
                              jnp.broadcast_to(zeta[:, :, None], (N_RET_HEADS,) + full),
                              jnp.broadcast_to(g_chunk[:, None, None], (N_RET_HEADS,) + full)], axis=1))
    return jnp.stack(out, axis=0)


def _na_bias_table(rpb):
    rr = jnp.arange(NA_ROWS)
    kr = jnp.arange(NA_WIN)
    rel_r = jnp.stack([rr, rr + NA_KH // 2, rr + NA_WIN - NA_ROWS])
    rel_r0 = jnp.stack([jnp.zeros_like(rr), rr, jnp.full_like(rr, NA_WIN - NA_KH)])
    row_ok = (kr[None, None, :] >= rel_r0[:, :, None]) & (kr[None, None, :] < rel_r0[:, :, None] + NA_KH)
    dr = jnp.clip(kr[None, None, :] - rel_r[:, :, None] + (NA_KH - 1), 0, 2 * NA_KH - 2)
    cq = jnp.arange(GRID_W)
    c0 = jnp.clip(cq - NA_KW // 2, 0, GRID_W - NA_KW)
    col_ok = (cq[None, :] >= c0[:, None]) & (cq[None, :] < c0[:, None] + NA_KW)
    dc = jnp.clip(cq[None, :] - cq[:, None] + (NA_KW - 1), 0, 2 * NA_KW - 2)
    bias = rpb.astype(F32)[:, dr[:, :, :, None, None], dc[None, None, None, :, :]]
    ok = row_ok[:, :, :, None, None] & col_ok[None, None, None, :, :]
    bias = jnp.where(ok[None], bias, NEG_INF)
    h = rpb.shape[0]
    return bias.transpose(0, 1, 2, 4, 3, 5).reshape(h, 3, NA_ROWS * GRID_W, NA_WIN * GRID_W)


def kernel(x_prompt, x_sample, c, state_ret, cache_na_k, cache_na_v, c_ctx, norm_g, w_mod, b_mod, ffn_w1, ffn_w2,
           w_in, ret_decay_logit, conv_w, conv_b, na_rpb, w_branch, w_merge, b_merge, w_out, final_g):
    batch, seq, d = x_prompt.shape
    dec_batch, dec_seq, _ = x_sample.shape
    depth = norm_g.shape[0]
    past_len = cache_na_k.shape[3]
    n_ctx = batch * seq
    n_lat = dec_batch * dec_seq
    n_tok = n_ctx + n_lat
    tm = TOKEN_BLOCK
    n_rows = dec_seq // GRID_W
    assert d == D_MODEL and 1 + dec_batch <= COND_ROWS
    assert n_ctx % tm == 0 and dec_seq % tm == 0 and tm % seq == 0
    assert seq & (seq - 1) == 0 and dec_seq & (dec_seq - 1) == 0
    assert seq % RET_CHUNK == 0 and dec_seq % RET_CHUNK == 0 and n_ctx % dec_seq == 0
    assert n_rows % NA_ROWS == 0 and n_rows >= NA_WIN and D_FF % FF_CHUNK == 0
    n_blocks = n_tok // tm
    ctx_blocks = n_ctx // tm
    nj = D_FF // FF_CHUNK

    x = jnp.concatenate([x_prompt.reshape(n_ctx, d), x_sample.reshape(n_lat, d)], axis=0)
    cond = jnp.zeros((COND_ROWS, d), F32).at[0].set(c_ctx).at[1:1 + dec_batch].set(c)
    mod = _mod_call(cond, w_mod, b_mod).reshape(depth, COND_ROWS, N_MOD, d)

    w1c = (ffn_w1.astype(BF16).reshape(depth, 2, d, 2, nj, FF_CHUNK).transpose(0, 1, 3, 4, 2, 5))
    w2c = ffn_w2.astype(BF16).reshape(depth, 2, nj, FF_CHUNK, d)
    w_in_b = w_in.astype(BF16)
    w_merge_b = w_merge.astype(BF16)
    w_branch_b = w_branch.astype(BF16)
    w_out_b = w_out.astype(BF16)
    b_merge3 = b_merge.reshape(depth, 1, N_BRANCH * d)
    conv_b3 = conv_b.reshape(depth, 1, BRANCH_W)
    rope_cos, rope_sin = _rope_tables(dec_seq, RET_DK)
    n_chunks_lat = dec_seq // RET_CHUNK
    n_chunks_ctx = seq // RET_CHUNK
    na_blocks = n_rows // NA_ROWS

    def mod_spec(l):
        def index(i):
            row = i * tm
            return (l, jnp.where(row < n_ctx, 0, 1 + (row - n_ctx) // dec_seq), 0, 0)
        return pl.BlockSpec((None, None, N_MOD, d), index)

    def g_spec(l):
        return pl.BlockSpec((None, 3, d), lambda i: (l, 0, 0))

    x_spec = pl.BlockSpec((tm, d), lambda i: (i, 0))
    x_shape = jax.ShapeDtypeStruct((n_tok, d), F32)

    def ffn(xv, l, s):
        return pl.pallas_call(
            functools.partial(_ffn_kernel, gi=2 * s, mi=6 * s, nj=nj),
            grid=(n_blocks,),
            in_specs=[x_spec, mod_spec(l), g_spec(l),
                      _resident((None, None, 2, nj, d, FF_CHUNK), lambda i: (l, s, 0, 0, 0, 0)),
                      _resident((None, None, nj, FF_CHUNK, d), lambda i: (l, s, 0, 0, 0))],
            out_specs=x_spec,
            out_shape=x_shape,
            scratch_shapes=[pltpu.VMEM((tm, d), BF16), pltpu.VMEM((tm, d), F32)],
            compiler_params=_params(1),
            name="ffn",
        )(xv, mod, norm_g, w1c, w2c)

    def mixin(xv, l):
        return pl.pallas_call(
            _mixin_kernel,
            grid=(n_blocks,),
            in_specs=[x_spec, mod_spec(l), g_spec(l),
                      _resident((None, d, N_SPLIT * BRANCH_W), lambda i: (l, 0, 0))],
            out_specs=pl.BlockSpec((tm, N_SPLIT * BRANCH_W), lambda i: (i, 0)),
            out_shape=jax.ShapeDtypeStruct((n_tok, N_SPLIT * BRANCH_W), F32),
            compiler_params=_params(1),
            name="mixer_in",
        )(xv, mod, norm_g, w_in_b)

    def ret_ctx(proj, dec):
        col = lambda k: pl.BlockSpec((seq, BRANCH_W), lambda b: (b, k))
        return pl.pallas_call(
            functools.partial(_ret_ctx_kernel, n_chunks=n_chunks_ctx),
            grid=(batch,),
            in_specs=[col(0), col(1), col(2), col(3),
                      _resident((2, N_RET_HEADS, 4, RET_CHUNK, RET_CHUNK), lambda b: (0, 0, 0, 0, 0))],
            out_specs=[pl.BlockSpec((seq, BRANCH_W), lambda b: (b, 0)),
                       pl.BlockSpec((None, 2, N_RET_HEADS, RET_DK, RET_DV), lambda b: (b, 0, 0, 0, 0))],
            out_shape=[jax.ShapeDtypeStruct((n_ctx, BRANCH_W), F32),
                       jax.ShapeDtypeStruct((batch, 2, N_RET_HEADS, RET_DK, RET_DV), F32)],
            compiler_params=_params(1),
            name="retention_ctx",
        )(proj, proj, proj, proj, dec)

    def ret_lat(proj, dec, l):
        base = n_ctx // RET_CHUNK

        def chunk_of(p, ci):
            return jnp.where(p == 0, ci, n_chunks_lat - 1 - ci)

        col = lambda k: pl.BlockSpec((RET_CHUNK, BRANCH_W),
                                     lambda b, p, ci: (base + b * n_chunks_lat + chunk_of(p, ci), k))
        tab = pl.BlockSpec((RET_CHUNK, RET_DK), lambda b, p, ci: (chunk_of(p, ci), 0))
        return pl.pallas_call(
            functools.partial(_ret_lat_kernel, n_chunks=n_chunks_lat),
            grid=(dec_batch, 2, n_chunks_lat),
            in_specs=[col(0), col(1), col(2), col(3), tab, tab,
                      pl.BlockSpec((None, None, None, N_RET_HEADS, RET_DK, RET_DV),
                                   lambda b, p, ci: (b, l, p, 0, 0, 0)),
                      pl.BlockSpec((None, N_RET_HEADS, 4, RET_CHUNK, RET_CHUNK),
                                   lambda b, p, ci: (p, 0, 0, 0, 0))],
            out_specs=pl.BlockSpec((RET_CHUNK, BRANCH_W),
                                   lambda b, p, ci: (b * n_chunks_lat + n_chunks_lat - 1 - jnp.where(p == 0, 0, ci),
                                                     0)),
            out_shape=jax.ShapeDtypeStruct((n_lat, BRANCH_W), F32),
            scratch_shapes=[pltpu.VMEM((N_RET_HEADS, RET_DK, RET_DV), F32),
                            pltpu.VMEM((dec_seq, BRANCH_W), F32)],
            compiler_params=_params(3),
            name="retention_lat",
        )(proj, proj, proj, proj, rope_cos, rope_sin, state_ret, dec)

    def na_ctx(proj):
        col = lambda k: pl.BlockSpec((seq, BRANCH_W), lambda b: (b, k))
        cache_spec = pl.BlockSpec((None, N_NA_HEADS, seq, NA_HEAD_DIM), lambda b: (b, 0, 0, 0))
        cache_shape = jax.ShapeDtypeStruct((batch, N_NA_HEADS, seq, NA_HEAD_DIM), F32)
        return pl.pallas_call(
            _ctx_attn_kernel,
            grid=(batch,),
            in_specs=[col(7), col(8), col(9)],
            out_specs=[pl.BlockSpec((seq, BRANCH_W), lambda b: (b, 0)), cache_spec, cache_spec],
            out_shape=[jax.ShapeDtypeStruct((n_ctx, BRANCH_W), F32), cache_shape, cache_shape],
            compiler_params=_params(1),
            name="attention_ctx",
        )(proj, proj, proj)

    def na_lat(proj, bias, l):
        q_rows = NA_ROWS * GRID_W
        pair_w = 2 * NA_HEAD_DIM
        pairs = N_NA_HEADS // 2
        q_base = n_ctx // q_rows
        kv_base = n_ctx // dec_seq
        cache_spec = pl.BlockSpec((None, None, 2, past_len, NA_HEAD_DIM), lambda b, hp, i: (b, l, hp, 0, 0))

        def pattern(i):
            return jnp.where(i == 0, 0, jnp.where(i == na_blocks - 1, 2, 1))

        return pl.pallas_call(
            functools.partial(_na_kernel, n_rows=n_rows),
            grid=(dec_batch, pairs, na_blocks),
            in_specs=[pl.BlockSpec((q_rows, pair_w), lambda b, hp, i: (q_base + b * na_blocks + i, 7 * pairs + hp)),
                      pl.BlockSpec((dec_seq, pair_w), lambda b, hp, i: (kv_base + b, 8 * pairs + hp)),
                      pl.BlockSpec((dec_seq, pair_w), lambda b, hp, i: (kv_base + b, 9 * pairs + hp)),
                      cache_spec, cache_spec,
                      pl.BlockSpec((2, None, q_rows, NA_WIN * GRID_W), lambda b, hp, i: (hp, pattern(i), 0, 0))],
            out_specs=pl.BlockSpec((q_rows, pair_w), lambda b, hp, i: (b * na_blocks + i, hp)),
            out_shape=jax.ShapeDtypeStruct((n_lat, BRANCH_W), F32),
            compiler_params=_params(3),
            name="attention_lat",
        )(proj, proj, proj, cache_na_k, cache_na_v, bias)

    def mixout(xv, proj, ret_c, ret_l, na_c, na_l, l):
        col = lambda k: pl.BlockSpec((tm, BRANCH_W), lambda i: (i, k))
        before = lambda k: pl.BlockSpec((HALO, BRANCH_W), lambda i: (jnp.maximum(i * (tm // HALO) - 1, 0), k))
        after = lambda k: pl.BlockSpec((HALO, BRANCH_W),
                                       lambda i: (jnp.minimum((i + 1) * (tm // HALO), n_tok // HALO - 1), k))
        ctx_rows = pl.BlockSpec((tm, BRANCH_W), lambda i: (jnp.minimum(i, ctx_blocks - 1), 0))
        lat_rows = pl.BlockSpec((tm, BRANCH_W), lambda i: (jnp.maximum(i - ctx_blocks, 0), 0))
        return pl.pallas_call(
            functools.partial(_mixout_kernel, tm=tm, n_ctx=n_ctx, seq=seq, dec_seq=dec_seq),
            grid=(n_blocks,),
            in_specs=[x_spec, mod_spec(l), g_spec(l), col(4), col(5), col(6), before(5), before(6), after(5),
                      after(6), ctx_rows, lat_rows, ctx_rows, lat_rows,
                      pl.BlockSpec((None, CONV_K, BRANCH_W), lambda i: (l, 0, 0)),
                      pl.BlockSpec((None, 1, BRANCH_W), lambda i: (l, 0, 0)),
                      _resident((None, d, N_BRANCH * d), lambda i: (l, 0, 0)),
                      pl.BlockSpec((None, 1, N_BRANCH * d), lambda i: (l, 0, 0)),
                      _resident((None, N_BRANCH, BRANCH_W, d), lambda i: (l, 0, 0, 0)),
                      _resident((None, d, d), lambda i: (l, 0, 0))],
            out_specs=x_spec,
            out_shape=x_shape,
            compiler_params=_params(1),
            name="mixer_out",
        )(xv, mod, norm_g, proj, proj, proj, proj, proj, proj, proj, ret_c, ret_l, na_c, na_l, conv_w, conv_b3,
          w_merge_b, b_merge3, w_branch_b, w_out_b)

    def final(xv, first_block, n_rows_out):
        return pl.pallas_call(
            _final_kernel,
            grid=(n_rows_out // tm,),
            in_specs=[pl.BlockSpec((tm, d), lambda i: (first_block + i, 0)),
                      pl.BlockSpec((1, d), lambda i: (0, 0))],
            out_specs=pl.BlockSpec((tm, d), lambda i: (i, 0)),
            out_shape=jax.ShapeDtypeStruct((n_rows_out, d), F32),
            compiler_params=_params(1),
            name="final_norm",
        )(xv, final_g.reshape(1, d))

    states, cache_k, cache_v = [], [], []
    for l in range(depth):
        x = ffn(x, l, 0)
        proj = mixin(x, l)
        dec = _decay_tables(ret_decay_logit[l])
        ret_c, st = ret_ctx(proj, dec)
        ret_l = ret_lat(proj, dec, l)
        na_c, ck, cv = na_ctx(proj)
        na_l = na_lat(proj, _na_bias_table(na_rpb[l]), l)
        x = mixout(x, proj, ret_c, ret_l, na_c, na_l, l)
        x = ffn(x, l, 1)
        states.append(st)
        cache_k.append(ck)
        cache_v.append(cv)

    y_prompt = final(x, 0, n_ctx).reshape(batch, seq, d)
    y_sample = final(x, ctx_blocks, n_lat).reshape(dec_batch, dec_seq, d)
    return (y_prompt, y_sample, jnp.stack(states, axis=1), jnp.stack(cache_k, axis=1), jnp.stack(cache_v, axis=1))
```

```python
import functools

import jax
import jax.numpy as jnp
import numpy as np
from jax import lax
from jax.experimental import pallas as pl
from jax.experimental.pallas import tpu as pltpu

D_MODEL = 1024
GRID_W = 64
N_RET_HEADS = 4
RET_DK = 128
RET_DV = 128
RET_CHUNK = 128
CONV_K = 3
N_NA_HEADS = 8
NA_HEAD_DIM = 64
NA_KH = 8
NA_KW = 16
BRANCH_W = 512
N_BRANCH = 3
N_SPLIT = 10
D_FF = 2816
N_MOD = 9
ROPE_BASE = 10000.0
EPS = 1e-6
NEG_INF = -1e30

BF16 = jnp.bfloat16
F32 = jnp.float32

VMEM_LIMIT_BYTES = 56 * 1024 * 1024
LANES = 128
COND_ROWS = 8
TOKEN_BLOCK = 512
FF_CHUNK = 256
NA_ROWS = 4
NA_WIN = NA_ROWS + NA_KH - 1
NA_SUB = 4
HALO = 8


def _params(n_axes):
    return pltpu.CompilerParams(dimension_semantics=("arbitrary",) * n_axes,
                                vmem_limit_bytes=VMEM_LIMIT_BYTES)


def _resident(block_shape, index_map):
    return pl.BlockSpec(block_shape, index_map, pipeline_mode=pl.Buffered(1))


def _dot(a, b):
    return jnp.dot(a, b, preferred_element_type=F32)


def _dot_nt(a, b):
    return lax.dot_general(a, b, (((1,), (1,)), ((), ())), preferred_element_type=F32)


def _dot_tn(a, b):
    return lax.dot_general(a, b, (((0,), (0,)), ((), ())), preferred_element_type=F32)


def _norm_mod(x, g_row, scale_row, shift_row):
    ms = jnp.mean(x * x, axis=-1, keepdims=True)
    y = x * lax.rsqrt(ms + EPS)
    return (y * g_row) * (1.0 + scale_row) + shift_row


def _mod_kernel(cond_ref, w_ref, b_ref, o_ref):
    a = jax.nn.silu(cond_ref[...]).astype(BF16)
    o_ref[...] = _dot(a, w_ref[...].astype(BF16)) + b_ref[...]


def _mod_call(cond, w_mod, b_mod):
    depth, d, n = w_mod.shape
    tn = 1024
    return pl.pallas_call(
        _mod_kernel,
        grid=(depth, n // tn),
        in_specs=[pl.BlockSpec((COND_ROWS, d), lambda l, j: (0, 0)),
                  pl.BlockSpec((None, d, tn), lambda l, j: (l, 0, j)),
                  pl.BlockSpec((None, 1, tn), lambda l, j: (l, 0, j))],
        out_specs=pl.BlockSpec((None, COND_ROWS, tn), lambda l, j: (l, 0, j)),
        out_shape=jax.ShapeDtypeStruct((depth, COND_ROWS, n), F32),
        compiler_params=_params(2),
        name="mod_vectors",
    )(cond, w_mod, b_mod.reshape(depth, 1, n))


def _ffn_kernel(x_ref, mod_ref, g_ref, w1_ref, w2_ref, o_ref, hb_ref, t_ref, *, gi, mi):
    h = _norm_mod(x_ref[...], g_ref[gi:gi + 1, :], mod_ref[mi + 1:mi + 2, :], mod_ref[mi:mi + 1, :])
    hb_ref[...] = h.astype(BF16)
    for j in range(D_FF // FF_CHUNK):
        hb = hb_ref[...]
        a = _dot(hb, w1_ref[:, j * FF_CHUNK:(j + 1) * FF_CHUNK])
        b = _dot(hb, w1_ref[:, D_FF + j * FF_CHUNK:D_FF + (j + 1) * FF_CHUNK])
        t_ref[:, j * FF_CHUNK:(j + 1) * FF_CHUNK] = (jax.nn.silu(a) * b).astype(BF16)
    o_ref[...] = x_ref[...] + (0.5 * mod_ref[mi + 2:mi + 3, :]) * _dot(t_ref[...], w2_ref[...])


def _mixin_kernel(x_ref, mod_ref, g_ref, w_ref, cos_ref, sin_ref, o_ref):
    u = _norm_mod(x_ref[...], g_ref[1:2, :], mod_ref[4:5, :], mod_ref[3:4, :]).astype(BF16)
    cos = cos_ref[...]
    sin = sin_ref[...]
    lane = lax.broadcasted_iota(jnp.int32, cos.shape, 1)
    first = (lane % (RET_DK // 2)) < (RET_DK // 4)
    for k in range(N_SPLIT):
        sl = slice(k * BRANCH_W, (k + 1) * BRANCH_W)
        r = _dot(u, w_ref[:, sl])
        if k < 2:
            heads = []
            for h in range(N_RET_HEADS):
                xh = r[:, h * RET_DK:(h + 1) * RET_DK]
                if k == 0:
                    xh = xh * (RET_DK ** -0.5)
                swapped = jnp.where(first, pltpu.roll(xh, RET_DK - RET_DK // 4, 1), pltpu.roll(xh, RET_DK // 4, 1))
                heads.append(xh * cos + swapped * sin)
            r = jnp.concatenate(heads, axis=-1)
        o_ref[:, sl] = r


def _decayed_keys_t(k_tiles, zeta_tiles):
    kz = jnp.concatenate([k * z for k, z in zip(k_tiles, zeta_tiles)], axis=-1)
    kzt = kz.T.astype(BF16)
    return [kzt[i * RET_DK:(i + 1) * RET_DK] for i in range(len(k_tiles))]


def _ret_level(items):
    stage = []
    for q, k, kzt, v, s, dm, xi, gc in items:
        qb = q.astype(BF16)
        vb = v.astype(BF16)
        stage.append((vb, s, dm, xi, gc, _dot_nt(qb, k.astype(BF16)), _dot(qb, s.astype(BF16)), _dot(kzt, vb)))
    out = []
    for vb, s, dm, xi, gc, qk, qs, kv in stage:
        s_new = s * gc + kv
        out.append((_dot((qk * dm).astype(BF16), vb) + qs * xi, s_new))
    return out


def _head_norm_gate(os, gs):
    mus = [jnp.mean(o, axis=-1, keepdims=True) for o in os]
    ds = [o - mu for o, mu in zip(os, mus)]
    vs = [jnp.mean(jnp.square(d), axis=-1, keepdims=True) for d in ds]
    return [(d * lax.rsqrt(v + EPS)) * jax.nn.silu(g) for d, v, g in zip(ds, vs, gs)]


def _ret_ctx_kernel(q_ref, k_ref, v_ref, g_ref, dec_ref, st_in_ref, o_ref, st_ref, *, n_chunks):
    del st_in_ref
    c = RET_CHUNK
    heads = [slice(h * RET_DK, (h + 1) * RET_DK) for h in range(N_RET_HEADS)]
    rows = [slice(ci * c, (ci + 1) * c) for ci in range(n_chunks)]
    chains = [(d, h) for d in range(2) for h in range(N_RET_HEADS)]
    state = {dh: jnp.zeros((RET_DK, RET_DV), F32) for dh in chains}
    o_tot = {}
    for t in range(n_chunks):
        cis = {(d, h): (t if d == 0 else n_chunks - 1 - t) for d, h in chains}
        kzt = _decayed_keys_t([k_ref[rows[cis[d, h]], heads[h]] for d, h in chains],
                              [dec_ref[d, h, 2] for d, h in chains])
        items = [(q_ref[rows[cis[d, h]], heads[h]], k_ref[rows[cis[d, h]], heads[h]], kzt_dh,
                  v_ref[rows[cis[d, h]], heads[h]], state[d, h], dec_ref[d, h, 0], dec_ref[d, h, 1], dec_ref[d, h, 3])
                 for (d, h), kzt_dh in zip(chains, kzt)]
        for (d, h), (o, s_new) in zip(chains, _ret_level(items)):
            state[d, h] = s_new
            key = (cis[d, h], h)
            o_tot[key] = o if key not in o_tot else o_tot[key] + o
    for d, h in chains:
        st_ref[d, h] = state[d, h]
    tiles = [(ci, h) for ci in range(n_chunks) for h in range(N_RET_HEADS)]
    normed = _head_norm_gate([o_tot[t] for t in tiles], [g_ref[rows[ci], heads[h]] for ci, h in tiles])
    for (ci, h), y in zip(tiles, normed):
        o_ref[rows[ci], heads[h]] = y


def _ret_lat_kernel(q_ref, k_ref, v_ref, g_ref, s0_ref, dec_ref, o_ref, s_scr, of_scr, *, n_chunks, n_seqs):
    p = pl.program_id(0)
    ci = pl.program_id(1)

    @pl.when(ci == 0)
    def _():
        s_scr[...] = s0_ref[...]

    chunk = jnp.where(p == 0, ci, n_chunks - 1 - ci)
    row0 = pl.multiple_of(chunk * RET_CHUNK, RET_CHUNK)
    heads = [slice(h * RET_DK, (h + 1) * RET_DK) for h in range(N_RET_HEADS)]
    tiles = [(b, h) for b in range(n_seqs) for h in range(N_RET_HEADS)]
    kzt = _decayed_keys_t([k_ref[b, :, heads[h]] for b, h in tiles], [dec_ref[h, 2] for _, h in tiles])
    items = [(q_ref[b, :, heads[h]], k_ref[b, :, heads[h]], kzt_bh, v_ref[b, :, heads[h]], s_scr[b, h],
              dec_ref[h, 0], dec_ref[h, 1], dec_ref[h, 3]) for (b, h), kzt_bh in zip(tiles, kzt)]
    os = []
    for (b, h), (o, s_new) in zip(tiles, _ret_level(items)):
        s_scr[b, h] = s_new
        os.append(o)

    @pl.when(p == 0)
    def _():
        for (b, h), o in zip(tiles, os):
            of_scr[b, pl.ds(row0, RET_CHUNK), heads[h]] = o

    @pl.when(p == 1)
    def _():
        tots = [of_scr[b, pl.ds(row0, RET_CHUNK), heads[h]] + o for (b, h), o in zip(tiles, os)]
        normed = _head_norm_gate(tots, [g_ref[b, :, heads[h]] for b, h in tiles])
        for (b, h), y in zip(tiles, normed):
            o_ref[b, :, heads[h]] = y


def _softmax_parts(parts):
    m = functools.reduce(jnp.maximum, [jnp.max(s, axis=-1, keepdims=True) for s in parts])
    es = [jnp.exp(s - m) for s in parts]
    den = functools.reduce(lambda a, b: a + b, [jnp.sum(e, axis=-1, keepdims=True) for e in es])
    return es, 1.0 / den


def _split_pair(q2):
    low = lax.broadcasted_iota(jnp.int32, q2.shape, 1) < NA_HEAD_DIM
    return low, (jnp.where(low, q2, 0.0).astype(BF16), jnp.where(low, 0.0, q2).astype(BF16))


def _ctx_attn_kernel(q_ref, k_ref, v_ref, ck_in_ref, cv_in_ref, o_ref, ck_ref, cv_ref):
    del ck_in_ref, cv_in_ref
    for hp in range(N_NA_HEADS // 2):
        sl = slice(hp * LANES, (hp + 1) * LANES)
        k2 = k_ref[:, sl]
        v2 = v_ref[:, sl]
        for hh in range(2):
            ck_ref[2 * hp + hh] = k2[:, hh * NA_HEAD_DIM:(hh + 1) * NA_HEAD_DIM]
            cv_ref[2 * hp + hh] = v2[:, hh * NA_HEAD_DIM:(hh + 1) * NA_HEAD_DIM]
        kb = k2.astype(BF16)
        vb = v2.astype(BF16)
        low, q_heads = _split_pair(q_ref[:, sl] * (NA_HEAD_DIM ** -0.5))
        outs = []
        for qh in q_heads:
            (e,), inv = _softmax_parts([_dot_nt(qh, kb)])
            outs.append(_dot(e.astype(BF16), vb) * inv)
        o_ref[:, sl] = jnp.where(low, outs[0], outs[1])


def _na_kernel(q_ref, k_ref, v_ref, kc_ref, vc_ref, bias_ref, o_ref, *, n_rows):
    i = pl.program_id(2)
    q_rows = NA_ROWS * GRID_W
    n_groups = n_rows // NA_ROWS
    kcb = [kc_ref[hh].astype(BF16) for hh in range(2)]
    vcb = [vc_ref[hh].astype(BF16) for hh in range(2)]
    for j in range(NA_SUB):
        grp = i * NA_SUB + j
        pattern = jnp.where(grp == 0, 0, jnp.where(grp == n_groups - 1, 2, 1))
        win0 = jnp.clip(NA_ROWS * grp - NA_KH // 2, 0, n_rows - NA_WIN)
        start = pl.multiple_of(win0 * GRID_W, GRID_W)
        kw = k_ref[pl.ds(start, NA_WIN * GRID_W), :].astype(BF16)
        vw = v_ref[pl.ds(start, NA_WIN * GRID_W), :].astype(BF16)
        q2 = q_ref[j * q_rows:(j + 1) * q_rows, :] * (NA_HEAD_DIM ** -0.5)
        low, q_heads = _split_pair(q2)
        loc, ctx = [], []
        for hh in range(2):
            s_loc = _dot_nt(q_heads[hh], kw) + bias_ref[hh, pattern]
            s_ctx = _dot_nt(q2[:, hh * NA_HEAD_DIM:(hh + 1) * NA_HEAD_DIM].astype(BF16), kcb[hh])
            (e_loc, e_ctx), inv = _softmax_parts([s_loc, s_ctx])
            loc.append(_dot(e_loc.astype(BF16), vw) * inv)
            ctx.append(_dot(e_ctx.astype(BF16), vcb[hh]) * inv)
        o_ref[j * q_rows:(j + 1) * q_rows, :] = (jnp.where(low, loc[0], loc[1])
                                                 + jnp.concatenate(ctx, axis=-1))


def _mixout_kernel(x_ref, mod_ref, g_ref, cb_ref, cc_ref, ch_ref, ccp_ref, chp_ref, ccn_ref, chn_ref,
                   retc_ref, retl_ref, nac_ref, nal_ref, cw_ref, cbias_ref, wm_ref, bm_ref, wb_ref, wo_ref,
                   o_ref, mb_ref, *, tm, n_ctx, seq, dec_seq):
    i = pl.program_id(0)
    is_ctx = i * tm < n_ctx
    ub = _norm_mod(x_ref[...], g_ref[1:2, :], mod_ref[4:5, :], mod_ref[3:4, :]).astype(BF16)

    z = cc_ref[...] * ch_ref[...]
    z_before = (ccp_ref[...] * chp_ref[...])[HALO - 1:HALO, :]
    z_after = (ccn_ref[...] * chn_ref[...])[0:1, :]
    row = lax.broadcasted_iota(jnp.int32, z.shape, 0)
    last_pos = jnp.where(is_ctx, seq - 1, dec_seq - 1)
    pos = (i * tm - jnp.where(is_ctx, 0, n_ctx) + row) & last_pos
    z_prev = jnp.where(row == 0, z_before, pltpu.roll(z, 1, 0))
    z_prev = jnp.where(pos == 0, 0.0, z_prev)
    z_next = jnp.where(row == tm - 1, z_after, pltpu.roll(z, tm - 1, 0))
    z_next = jnp.where(pos == last_pos, 0.0, z_next)
    y = z_prev * cw_ref[0:1, :] + z * cw_ref[1:2, :] + z_next * cw_ref[2:3, :] + cbias_ref[...]
    conv_out = cb_ref[...] * y

    ret = jnp.where(is_ctx, retc_ref[...], retl_ref[...])
    na = jnp.where(is_ctx, nac_ref[...], nal_ref[...])
    branches = (ret.astype(BF16), conv_out.astype(BF16), na.astype(BF16))
    for c in range(D_MODEL // FF_CHUNK):
        merged = None
        for b in range(N_BRANCH):
            sl = slice(b * D_MODEL + c * FF_CHUNK, b * D_MODEL + (c + 1) * FF_CHUNK)
            gate = jax.nn.sigmoid(_dot(ub, wm_ref[:, sl]) + bm_ref[:, sl])
            term = gate * _dot(branches[b], wb_ref[b, :, c * FF_CHUNK:(c + 1) * FF_CHUNK])
            merged = term if merged is None else merged + term
        mb_ref[:, c * FF_CHUNK:(c + 1) * FF_CHUNK] = merged.astype(BF16)
    o_ref[...] = x_ref[...] + mod_ref[5:6, :] * _dot(mb_ref[...], wo_ref[...])


def _final_kernel(x_ref, g_ref, o_ref):
    x = x_ref[...]
    ms = jnp.mean(x * x, axis=-1, keepdims=True)
    o_ref[...] = (x * lax.rsqrt(ms + EPS)) * g_ref[...]


def _rope_tables(length, dim):
    t = jnp.arange(length)
    half = dim // 2
    quarter = half // 2
    inv_freq = ROPE_BASE ** (-jnp.arange(quarter, dtype=F32) * 2.0 / half)

    def tables(pos):
        ang = pos.astype(F32)[:, None] * inv_freq[None, :]
        cos, sin = jnp.cos(ang), jnp.sin(ang)
        return jnp.concatenate([cos, cos], axis=-1), jnp.concatenate([-sin, sin], axis=-1)

    cr, sr = tables(t // GRID_W)
    cc, sc = tables(t % GRID_W)
    return jnp.concatenate([cr, cc], axis=-1), jnp.concatenate([sr, sc], axis=-1)


def _decay_tables(decay_logit):
    c = RET_CHUNK
    log_g = jax.nn.log_sigmoid(decay_logit.astype(F32))
    idx = jnp.arange(c, dtype=F32)
    diff = idx[:, None] - idx[None, :]
    out = []
    for d in range(2):
        lg = log_g[d]
        decay_in = jnp.where(diff >= 0, jnp.exp(lg[:, None, None] * jnp.maximum(diff, 0.0)), 0.0)
        xi = jnp.exp(lg[:, None] * (idx + 1.0))
        zeta = jnp.exp(lg[:, None] * (c - 1.0 - idx))
        g_chunk = jnp.exp(lg * c)
        if d == 1:
            decay_in = jnp.swapaxes(decay_in, 1, 2)
            xi = xi[:, ::-1]
            zeta = zeta[:, ::-1]
        full = (c, c)
        out.append(jnp.stack([decay_in,
                              jnp.broadcast_to(xi[:, :, None], (N_RET_HEADS,) + full),
                              jnp.broadcast_to(zeta[:, :, None], (N_RET_HEADS,) + full),
                              jnp.broadcast_to(g_chunk[:, None, None], (N_RET_HEADS,) + full)], axis=1))
    return jnp.stack(out, axis=0)


def _na_bias_table(rpb):
    h, n_dr, n_dc = rpb.shape
    w = GRID_W
    lead = w - NA_KW
    g = jnp.pad(rpb.astype(F32), ((0, 0), (0, 0), (lead, 2 * w - lead - n_dc)), constant_values=NEG_INF)
    flat = jnp.broadcast_to(g[:, :, None, :], (h, n_dr, w, 2 * w)).reshape(h, n_dr, 2 * w * w)
    tiles = flat[:, :, w - 1:w - 1 + w * (2 * w - 1)].reshape(h, n_dr, w, 2 * w - 1)[:, :, :, :w]
    cq = np.arange(w)
    c0 = np.clip(cq - NA_KW // 2, 0, w - NA_KW)
    col_ok = (cq[None, :] >= c0[:, None]) & (cq[None, :] < c0[:, None] + NA_KW)
    tiles = jnp.where(col_ok[None, None], tiles, NEG_INF)
    tiles = jnp.concatenate([tiles, jnp.full((h, 1, w, w), NEG_INF, F32)], axis=1)
    rr = np.arange(NA_ROWS)
    kr = np.arange(NA_WIN)
    rel_r = np.stack([rr, rr + NA_KH // 2, rr + NA_WIN - NA_ROWS])
    rel_r0 = np.stack([np.zeros_like(rr), rr, np.full_like(rr, NA_WIN - NA_KH)])
    row_ok = (kr[None, None, :] >= rel_r0[:, :, None]) & (kr[None, None, :] < rel_r0[:, :, None] + NA_KH)
    dr = np.where(row_ok, kr[None, None, :] - rel_r[:, :, None] + (NA_KH - 1), n_dr)
    bias = jnp.stack([jnp.stack([jnp.concatenate([tiles[:, int(dr[p, r, k])] for k in range(NA_WIN)], axis=-1)
                                 for r in range(NA_ROWS)], axis=1) for p in range(3)], axis=1)
    return bias.reshape(h, 3, NA_ROWS * w, NA_WIN * w)


def kernel(x_prompt, x_sample, c, state_ret, cache_na_k, cache_na_v, c_ctx, norm_g, w_mod, b_mod, ffn_w1, ffn_w2,
           w_in, ret_decay_logit, conv_w, conv_b, na_rpb, w_branch, w_merge, b_merge, w_out, final_g):
    batch, seq, d = x_prompt.shape
    dec_batch, dec_seq, _ = x_sample.shape
    depth = norm_g.shape[0]
    past_len = cache_na_k.shape[3]
    n_ctx = batch * seq
    n_lat = dec_batch * dec_seq
    n_tok = n_ctx + n_lat
    tm = TOKEN_BLOCK
    n_rows = dec_seq // GRID_W
    assert d == D_MODEL and 1 + dec_batch <= COND_ROWS
    assert n_ctx % tm == 0 and dec_seq % tm == 0 and tm % seq == 0
    assert seq & (seq - 1) == 0 and dec_seq & (dec_seq - 1) == 0
    assert seq % RET_CHUNK == 0 and dec_seq % RET_CHUNK == 0 and n_ctx % n_lat == 0
    assert n_rows % (NA_ROWS * NA_SUB) == 0 and n_rows >= NA_WIN and D_FF % FF_CHUNK == 0
    n_blocks = n_tok // tm
    ctx_blocks = n_ctx // tm

    x = jnp.concatenate([x_prompt.reshape(n_ctx, d), x_sample.reshape(n_lat, d)], axis=0)
    cond = jnp.zeros((COND_ROWS, d), F32).at[0].set(c_ctx).at[1:1 + dec_batch].set(c)
    mod = _mod_call(cond, w_mod, b_mod).reshape(depth, COND_ROWS, N_MOD, d)

    w1_b = ffn_w1.astype(BF16)
    w2_b = ffn_w2.astype(BF16)
    w_in_b = w_in.astype(BF16)
    w_merge_b = w_merge.astype(BF16)
    w_branch_b = w_branch.astype(BF16)
    w_out_b = w_out.astype(BF16)
    b_merge3 = b_merge.reshape(depth, 1, N_BRANCH * d)
    conv_b3 = conv_b.reshape(depth, 1, BRANCH_W)
    rope_cos, rope_sin = _rope_tables(dec_seq, RET_DK)
    rope_cos = jnp.concatenate([jnp.ones((tm, RET_DK), F32), rope_cos], axis=0)
    rope_sin = jnp.concatenate([jnp.zeros((tm, RET_DK), F32), rope_sin], axis=0)
    n_chunks_lat = dec_seq // RET_CHUNK
    n_chunks_ctx = seq // RET_CHUNK
    na_steps = n_rows // (NA_ROWS * NA_SUB)
    proj_w = N_SPLIT * BRANCH_W

    def mod_spec(l):
        def index(i):
            row = i * tm
            return (l, jnp.where(row < n_ctx, 0, 1 + (row - n_ctx) // dec_seq), 0, 0)
        return pl.BlockSpec((None, None, N_MOD, d), index)

    def g_spec(l):
        return pl.BlockSpec((None, 3, d), lambda i: (l, 0, 0))

    x_spec = pl.BlockSpec((tm, d), lambda i: (i, 0))
    x_shape = jax.ShapeDtypeStruct((n_tok, d), F32)
    hbm_spec = pl.BlockSpec(memory_space=pl.ANY)

    def ffn(xv, l, s):
        return pl.pallas_call(
            functools.partial(_ffn_kernel, gi=2 * s, mi=6 * s),
            grid=(n_blocks,),
            in_specs=[x_spec, mod_spec(l), g_spec(l),
                      _resident((None, None, d, 2 * D_FF), lambda i: (l, s, 0, 0)),
                      _resident((None, None, D_FF, d), lambda i: (l, s, 0, 0))],
            out_specs=x_spec,
            out_shape=x_shape,
            scratch_shapes=[pltpu.VMEM((tm, d), BF16), pltpu.VMEM((tm, D_FF), BF16)],
            compiler_params=_params(1),
            name="ffn",
        )(xv, mod, norm_g, w1_b, w2_b)

    def mixin(xv, l):
        def table_block(i):
            row = i * tm
            return (jnp.where(row < n_ctx, 0, 1 + ((row - n_ctx) % dec_seq) // tm), 0)

        tab = pl.BlockSpec((tm, RET_DK), table_block)
        return pl.pallas_call(
            _mixin_kernel,
            grid=(n_blocks,),
            in_specs=[x_spec, mod_spec(l), g_spec(l), _resident((None, d, proj_w), lambda i: (l, 0, 0)), tab, tab],
            out_specs=pl.BlockSpec((tm, proj_w), lambda i: (i, 0)),
            out_shape=jax.ShapeDtypeStruct((n_tok, proj_w), F32),
            compiler_params=_params(1),
            name="mixer_in",
        )(xv, mod, norm_g, w_in_b, rope_cos, rope_sin)

    def ret_ctx(proj, dec, states, l):
        col = lambda k: pl.BlockSpec((seq, BRANCH_W), lambda b: (b, k))
        return pl.pallas_call(
            functools.partial(_ret_ctx_kernel, n_chunks=n_chunks_ctx),
            grid=(batch,),
            in_specs=[col(0), col(1), col(2), col(3),
                      _resident((2, N_RET_HEADS, 4, RET_CHUNK, RET_CHUNK), lambda b: (0, 0, 0, 0, 0)), hbm_spec],
            out_specs=[pl.BlockSpec((seq, BRANCH_W), lambda b: (b, 0)),
                       pl.BlockSpec((None, None, 2, N_RET_HEADS, RET_DK, RET_DV), lambda b: (b, l, 0, 0, 0, 0))],
            out_shape=[jax.ShapeDtypeStruct((n_ctx, BRANCH_W), F32),
                       jax.ShapeDtypeStruct(states.shape, F32)],
            input_output_aliases={5: 1},
            compiler_params=_params(1),
            name="retention_ctx",
        )(proj, proj, proj, proj, dec, states)

    def ret_lat(proj, dec, l):
        proj4 = proj.reshape(n_tok // n_lat, dec_batch, dec_seq, proj_w)
        lat = n_ctx // n_lat

        def chunk_of(p, ci):
            return jnp.where(p == 0, ci, n_chunks_lat - 1 - ci)

        col = lambda k: pl.BlockSpec((None, dec_batch, RET_CHUNK, BRANCH_W),
                                     lambda p, ci: (lat, 0, chunk_of(p, ci), k))
        return pl.pallas_call(
            functools.partial(_ret_lat_kernel, n_chunks=n_chunks_lat, n_seqs=dec_batch),
            grid=(2, n_chunks_lat),
            in_specs=[col(0), col(1), col(2), col(3),
                      pl.BlockSpec((dec_batch, None, None, N_RET_HEADS, RET_DK, RET_DV),
                                   lambda p, ci: (0, l, p, 0, 0, 0)),
                      pl.BlockSpec((None, N_RET_HEADS, 4, RET_CHUNK, RET_CHUNK), lambda p, ci: (p, 0, 0, 0, 0))],
            out_specs=pl.BlockSpec((dec_batch, RET_CHUNK, BRANCH_W),
                                   lambda p, ci: (0, n_chunks_lat - 1 - jnp.where(p == 0, 0, ci), 0)),
            out_shape=jax.ShapeDtypeStruct((dec_batch, dec_seq, BRANCH_W), F32),
            scratch_shapes=[pltpu.VMEM((dec_batch, N_RET_HEADS, RET_DK, RET_DV), F32),
                            pltpu.VMEM((dec_batch, dec_seq, BRANCH_W), F32)],
            compiler_params=_params(2),
            name="retention_lat",
        )(proj4, proj4, proj4, proj4, state_ret, dec).reshape(n_lat, BRANCH_W)

    def na_ctx(proj, ck, cv, l):
        col = lambda k: pl.BlockSpec((seq, BRANCH_W), lambda b: (b, k))
        cache_spec = pl.BlockSpec((None, None, N_NA_HEADS, seq, NA_HEAD_DIM), lambda b: (b, l, 0, 0, 0))
        cache_shape = jax.ShapeDtypeStruct(ck.shape, F32)
        return pl.pallas_call(
            _ctx_attn_kernel,
            grid=(batch,),
            in_specs=[col(7), col(8), col(9), hbm_spec, hbm_spec],
            out_specs=[pl.BlockSpec((seq, BRANCH_W), lambda b: (b, 0)), cache_spec, cache_spec],
            out_shape=[jax.ShapeDtypeStruct((n_ctx, BRANCH_W), F32), cache_shape, cache_shape],
            input_output_aliases={3: 1, 4: 2},
            compiler_params=_params(1),
            name="attention_ctx",
        )(proj, proj, proj, ck, cv)

    def na_lat(proj, bias, l):
        q_rows = NA_SUB * NA_ROWS * GRID_W
        pairs = N_NA_HEADS // 2
        q_base = n_ctx // q_rows
        kv_base = n_ctx // dec_seq
        cache_spec = pl.BlockSpec((None, None, 2, past_len, NA_HEAD_DIM), lambda b, hp, i: (b, l, hp, 0, 0))
        return pl.pallas_call(
            functools.partial(_na_kernel, n_rows=n_rows),
            grid=(dec_batch, pairs, na_steps),
            in_specs=[pl.BlockSpec((q_rows, LANES), lambda b, hp, i: (q_base + b * na_steps + i, 7 * pairs + hp)),
                      pl.BlockSpec((dec_seq, LANES), lambda b, hp, i: (kv_base + b, 8 * pairs + hp)),
                      pl.BlockSpec((dec_seq, LANES), lambda b, hp, i: (kv_base + b, 9 * pairs + hp)),
                      cache_spec, cache_spec,
                      pl.BlockSpec((2, 3, NA_ROWS * GRID_W, NA_WIN * GRID_W), lambda b, hp, i: (hp, 0, 0, 0))],
            out_specs=pl.BlockSpec((q_rows, LANES), lambda b, hp, i: (b * na_steps + i, hp)),
            out_shape=jax.ShapeDtypeStruct((n_lat, BRANCH_W), F32),
            compiler_params=_params(3),
            name="attention_lat",
        )(proj, proj, proj, cache_na_k, cache_na_v, bias)

    def mixout(xv, proj, ret_c, ret_l, na_c, na_l, l):
        col = lambda k: pl.BlockSpec((tm, BRANCH_W), lambda i: (i, k))
        before = lambda k: pl.BlockSpec((HALO, BRANCH_W), lambda i: (jnp.maximum(i * (tm // HALO) - 1, 0), k))
        after = lambda k: pl.BlockSpec((HALO, BRANCH_W),
                                       lambda i: (jnp.minimum((i + 1) * (tm // HALO), n_tok // HALO - 1), k))
        ctx_rows = pl.BlockSpec((tm, BRANCH_W), lambda i: (jnp.minimum(i, ctx_blocks - 1), 0))
        lat_rows = pl.BlockSpec((tm, BRANCH_W), lambda i: (jnp.maximum(i - ctx_blocks, 0), 0))
        return pl.pallas_call(
            functools.partial(_mixout_kernel, tm=tm, n_ctx=n_ctx, seq=seq, dec_seq=dec_seq),
            grid=(n_blocks,),
            in_specs=[x_spec, mod_spec(l), g_spec(l), col(4), col(5), col(6), before(5), before(6), after(5),
                      after(6), ctx_rows, lat_rows, ctx_rows, lat_rows,
                      pl.BlockSpec((None, CONV_K, BRANCH_W), lambda i: (l, 0, 0)),
                      pl.BlockSpec((None, 1, BRANCH_W), lambda i: (l, 0, 0)),
                      _resident((None, d, N_BRANCH * d), lambda i: (l, 0, 0)),
                      pl.BlockSpec((None, 1, N_BRANCH * d), lambda i: (l, 0, 0)),
                      _resident((None, N_BRANCH, BRANCH_W, d), lambda i: (l, 0, 0, 0)),
                      _resident((None, d, d), lambda i: (l, 0, 0))],
            out_specs=x_spec,
            out_shape=x_shape,
            scratch_shapes=[pltpu.VMEM((tm, d), BF16)],
            compiler_params=_params(1),
            name="mixer_out",
        )(xv, mod, norm_g, proj, proj, proj, proj, proj, proj, proj, ret_c, ret_l, na_c, na_l, conv_w, conv_b3,
          w_merge_b, b_merge3, w_branch_b, w_out_b)

    def final(xv, first_block, n_rows_out):
        return pl.pallas_call(
            _final_kernel,
            grid=(n_rows_out // tm,),
            in_specs=[pl.BlockSpec((tm, d), lambda i: (first_block + i, 0)),
                      pl.BlockSpec((1, d), lambda i: (0, 0))],
            out_specs=pl.BlockSpec((tm, d), lambda i: (i, 0)),
            out_shape=jax.ShapeDtypeStruct((n_rows_out, d), F32),
            compiler_params=_params(1),
            name="final_norm",
        )(xv, final_g.reshape(1, d))

    states = jnp.zeros((batch, depth, 2, N_RET_HEADS, RET_DK, RET_DV), F32)
    cache_k = jnp.zeros((batch, depth, N_NA_HEADS, seq, NA_HEAD_DIM), F32)
    cache_v = jnp.zeros((batch, depth, N_NA_HEADS, seq, NA_HEAD_DIM), F32)
    for l in range(depth):
        x = ffn(x, l, 0)
        proj = mixin(x, l)
        dec = _decay_tables(ret_decay_logit[l])
        ret_c, states = ret_ctx(proj, dec, states, l)
        ret_l = ret_lat(proj, dec, l)
        na_c, cache_k, cache_v = na_ctx(proj, cache_k, cache_v, l)
        na_l = na_lat(proj, _na_bias_table(na_rpb[l]), l)
        x = mixout(x, proj, ret_c, ret_l, na_c, na_l, l)
        x = ffn(x, l, 1)

    y_prompt = final(x, 0, n_ctx).reshape(batch, seq, d)
    y_sample = final(x, ctx_blocks, n_lat).reshape(dec_batch, dec_seq, d)
    return (y_prompt, y_sample, states, cache_k, cache_v)
```

```python
import functools

import jax
import jax.numpy as jnp
import numpy as np
from jax import lax
from jax.experimental import pallas as pl
from jax.experimental.pallas import tpu as pltpu

D_MODEL = 1024
GRID_W = 64
N_RET_HEADS = 4
RET_DK = 128
RET_DV = 128
RET_CHUNK = 128
CONV_K = 3
N_NA_HEADS = 8
NA_HEAD_DIM = 64
NA_KH = 8
NA_KW = 16
BRANCH_W = 512
N_BRANCH = 3
N_SPLIT = 10
D_FF = 2816
N_MOD = 9
ROPE_BASE = 10000.0
EPS = 1e-6
NEG_INF = -1e30

BF16 = jnp.bfloat16
F32 = jnp.float32

VMEM_LIMIT_BYTES = 56 * 1024 * 1024
LANES = 128
COND_ROWS = 8
TOKEN_BLOCK = 512
FFN_BLOCK = 1024
FF_CHUNK = 256
NA_ROWS = 4
NA_WIN = NA_ROWS + NA_KH - 1
NA_SUB = 4
HALO = 8


def _params(n_axes):
    return pltpu.CompilerParams(dimension_semantics=("arbitrary",) * n_axes,
                                vmem_limit_bytes=VMEM_LIMIT_BYTES)


def _resident(block_shape, index_map):
    return pl.BlockSpec(block_shape, index_map, pipeline_mode=pl.Buffered(1))


def _dot(a, b):
    return jnp.dot(a, b, preferred_element_type=F32)


def _dot_nt(a, b):
    return lax.dot_general(a, b, (((1,), (1,)), ((), ())), preferred_element_type=F32)


def _dot_tn(a, b):
    return lax.dot_general(a, b, (((0,), (0,)), ((), ())), preferred_element_type=F32)


def _norm_mod(x, g_row, scale_row, shift_row):
    ms = jnp.mean(x * x, axis=-1, keepdims=True)
    y = x * lax.rsqrt(ms + EPS)
    return (y * g_row) * (1.0 + scale_row) + shift_row


def _mod_kernel(cond_ref, w_ref, b_ref, o_ref):
    a = jax.nn.silu(cond_ref[...]).astype(BF16)
    o_ref[...] = _dot(a, w_ref[...].astype(BF16)) + b_ref[...]


def _mod_call(cond, w_mod, b_mod):
    depth, d, n = w_mod.shape
    tn = 1024
    return pl.pallas_call(
        _mod_kernel,
        grid=(depth, n // tn),
        in_specs=[pl.BlockSpec((COND_ROWS, d), lambda l, j: (0, 0)),
                  pl.BlockSpec((None, d, tn), lambda l, j: (l, 0, j)),
                  pl.BlockSpec((None, 1, tn), lambda l, j: (l, 0, j))],
        out_specs=pl.BlockSpec((None, COND_ROWS, tn), lambda l, j: (l, 0, j)),
        out_shape=jax.ShapeDtypeStruct((depth, COND_ROWS, n), F32),
        compiler_params=_params(2),
        name="mod_vectors",
    )(cond, w_mod, b_mod.reshape(depth, 1, n))


def _ffn_kernel(x_ref, mod_ref, g_ref, w1_ref, w2_ref, o_ref, hb_ref, t_ref, *, gi, mi):
    h = _norm_mod(x_ref[...], g_ref[gi:gi + 1, :], mod_ref[mi + 1:mi + 2, :], mod_ref[mi:mi + 1, :])
    hb_ref[...] = h.astype(BF16)
    for j in range(D_FF // FF_CHUNK):
        hb = hb_ref[...]
        a = _dot(hb, w1_ref[:, j * FF_CHUNK:(j + 1) * FF_CHUNK])
        b = _dot(hb, w1_ref[:, D_FF + j * FF_CHUNK:D_FF + (j + 1) * FF_CHUNK])
        t_ref[:, j * FF_CHUNK:(j + 1) * FF_CHUNK] = (jax.nn.silu(a) * b).astype(BF16)
    o_ref[...] = x_ref[...] + (0.5 * mod_ref[mi + 2:mi + 3, :]) * _dot(t_ref[...], w2_ref[...])


def _mixin_kernel(x_ref, mod_ref, g_ref, w_ref, cos_ref, sin_ref, o_ref, u_ref):
    u = _norm_mod(x_ref[...], g_ref[1:2, :], mod_ref[4:5, :], mod_ref[3:4, :]).astype(BF16)
    u_ref[...] = u
    cos = cos_ref[...]
    sin = sin_ref[...]
    lane = lax.broadcasted_iota(jnp.int32, cos.shape, 1)
    first = (lane % (RET_DK // 2)) < (RET_DK // 4)
    for k in range(N_SPLIT):
        sl = slice(k * BRANCH_W, (k + 1) * BRANCH_W)
        r = _dot(u, w_ref[:, sl])
        if k < 2:
            heads = []
            for h in range(N_RET_HEADS):
                xh = r[:, h * RET_DK:(h + 1) * RET_DK]
                if k == 0:
                    xh = xh * (RET_DK ** -0.5)
                swapped = jnp.where(first, pltpu.roll(xh, RET_DK - RET_DK // 4, 1), pltpu.roll(xh, RET_DK // 4, 1))
                heads.append(xh * cos + swapped * sin)
            r = jnp.concatenate(heads, axis=-1)
        o_ref[:, sl] = r


def _decayed_keys_t(k_tiles, zeta_tiles):
    kz = jnp.concatenate([k * z for k, z in zip(k_tiles, zeta_tiles)], axis=-1)
    kzt = kz.T.astype(BF16)
    return [kzt[i * RET_DK:(i + 1) * RET_DK] for i in range(len(k_tiles))]


def _ret_level(items):
    stage = []
    for q, k, kzt, v, s, dm, xi, gc in items:
        qb = q.astype(BF16)
        vb = v.astype(BF16)
        stage.append((vb, s, dm, xi, gc, _dot_nt(qb, k.astype(BF16)), _dot(qb, s.astype(BF16)), _dot(kzt, vb)))
    out = []
    for vb, s, dm, xi, gc, qk, qs, kv in stage:
        s_new = s * gc + kv
        out.append((_dot((qk * dm).astype(BF16), vb) + qs * xi, s_new))
    return out


def _head_norm_gate(os, gs):
    mus = [jnp.mean(o, axis=-1, keepdims=True) for o in os]
    ds = [o - mu for o, mu in zip(os, mus)]
    vs = [jnp.mean(jnp.square(d), axis=-1, keepdims=True) for d in ds]
    return [(d * lax.rsqrt(v + EPS)) * jax.nn.silu(g) for d, v, g in zip(ds, vs, gs)]


def _ret_ctx_kernel(q_ref, k_ref, v_ref, g_ref, dec_ref, st_in_ref, o_ref, st_ref, *, n_chunks):
    del st_in_ref
    c = RET_CHUNK
    heads = [slice(h * RET_DK, (h + 1) * RET_DK) for h in range(N_RET_HEADS)]
    rows = [slice(ci * c, (ci + 1) * c) for ci in range(n_chunks)]
    chains = [(d, h) for d in range(2) for h in range(N_RET_HEADS)]
    state = {dh: jnp.zeros((RET_DK, RET_DV), F32) for dh in chains}
    o_tot = {}
    for t in range(n_chunks):
        cis = {(d, h): (t if d == 0 else n_chunks - 1 - t) for d, h in chains}
        kzt = _decayed_keys_t([k_ref[rows[cis[d, h]], heads[h]] for d, h in chains],
                              [dec_ref[d, h, 2] for d, h in chains])
        items = [(q_ref[rows[cis[d, h]], heads[h]], k_ref[rows[cis[d, h]], heads[h]], kzt_dh,
                  v_ref[rows[cis[d, h]], heads[h]], state[d, h], dec_ref[d, h, 0], dec_ref[d, h, 1], dec_ref[d, h, 3])
                 for (d, h), kzt_dh in zip(chains, kzt)]
        for (d, h), (o, s_new) in zip(chains, _ret_level(items)):
            state[d, h] = s_new
            key = (cis[d, h], h)
            o_tot[key] = o if key not in o_tot else o_tot[key] + o
    for d, h in chains:
        st_ref[d, h] = state[d, h]
    tiles = [(ci, h) for ci in range(n_chunks) for h in range(N_RET_HEADS)]
    normed = _head_norm_gate([o_tot[t] for t in tiles], [g_ref[rows[ci], heads[h]] for ci, h in tiles])
    for (ci, h), y in zip(tiles, normed):
        o_ref[rows[ci], heads[h]] = y.astype(o_ref.dtype)


def _ret_lat_kernel(q_ref, k_ref, v_ref, g_ref, s0_ref, dec_ref, o_ref, s_scr, of_scr, *, n_chunks, n_seqs):
    p = pl.program_id(0)
    ci = pl.program_id(1)

    @pl.when(ci == 0)
    def _():
        s_scr[...] = s0_ref[...]

    chunk = jnp.where(p == 0, ci, n_chunks - 1 - ci)
    row0 = pl.multiple_of(chunk * RET_CHUNK, RET_CHUNK)
    heads = [slice(h * RET_DK, (h + 1) * RET_DK) for h in range(N_RET_HEADS)]
    tiles = [(b, h) for b in range(n_seqs) for h in range(N_RET_HEADS)]
    kzt = _decayed_keys_t([k_ref[b, :, heads[h]] for b, h in tiles], [dec_ref[h, 2] for _, h in tiles])
    items = [(q_ref[b, :, heads[h]], k_ref[b, :, heads[h]], kzt_bh, v_ref[b, :, heads[h]], s_scr[b, h],
              dec_ref[h, 0], dec_ref[h, 1], dec_ref[h, 3]) for (b, h), kzt_bh in zip(tiles, kzt)]
    os = []
    for (b, h), (o, s_new) in zip(tiles, _ret_level(items)):
        s_scr[b, h] = s_new
        os.append(o)

    @pl.when(p == 0)
    def _():
        for (b, h), o in zip(tiles, os):
            of_scr[b, pl.ds(row0, RET_CHUNK), heads[h]] = o

    @pl.when(p == 1)
    def _():
        tots = [of_scr[b, pl.ds(row0, RET_CHUNK), heads[h]] + o for (b, h), o in zip(tiles, os)]
        normed = _head_norm_gate(tots, [g_ref[b, :, heads[h]] for b, h in tiles])
        for (b, h), y in zip(tiles, normed):
            o_ref[b, :, heads[h]] = y.astype(o_ref.dtype)


def _softmax_parts(parts):
    m = functools.reduce(jnp.maximum, [jnp.max(s, axis=-1, keepdims=True) for s in parts])
    es = [jnp.exp(s - m) for s in parts]
    den = functools.reduce(lambda a, b: a + b, [jnp.sum(e, axis=-1, keepdims=True) for e in es])
    return es, 1.0 / den


def _split_pair(q2):
    low = lax.broadcasted_iota(jnp.int32, q2.shape, 1) < NA_HEAD_DIM
    return low, (jnp.where(low, q2, 0.0).astype(BF16), jnp.where(low, 0.0, q2).astype(BF16))


def _ctx_attn_kernel(q_ref, k_ref, v_ref, ck_in_ref, cv_in_ref, o_ref, ck_ref, cv_ref):
    del ck_in_ref, cv_in_ref
    for hp in range(N_NA_HEADS // 2):
        sl = slice(hp * LANES, (hp + 1) * LANES)
        k2 = k_ref[:, sl]
        v2 = v_ref[:, sl]
        for hh in range(2):
            ck_ref[2 * hp + hh] = k2[:, hh * NA_HEAD_DIM:(hh + 1) * NA_HEAD_DIM]
            cv_ref[2 * hp + hh] = v2[:, hh * NA_HEAD_DIM:(hh + 1) * NA_HEAD_DIM]
        kb = k2.astype(BF16)
        vb = v2.astype(BF16)
        low, q_heads = _split_pair(q_ref[:, sl] * (NA_HEAD_DIM ** -0.5))
        outs = []
        for qh in q_heads:
            (e,), inv = _softmax_parts([_dot_nt(qh, kb)])
            outs.append(_dot(e.astype(BF16), vb) * inv)
        o_ref[:, sl] = jnp.where(low, outs[0], outs[1]).astype(o_ref.dtype)


def _na_kernel(q_ref, k_ref, v_ref, kc_ref, vc_ref, bias_ref, o_ref, *, n_rows):
    i = pl.program_id(2)
    q_rows = NA_ROWS * GRID_W
    n_groups = n_rows // NA_ROWS
    kcb = [kc_ref[hh].astype(BF16) for hh in range(2)]
    vcb = [vc_ref[hh].astype(BF16) for hh in range(2)]
    for j in range(NA_SUB):
        grp = i * NA_SUB + j
        pattern = jnp.where(grp == 0, 0, jnp.where(grp == n_groups - 1, 2, 1))
        win0 = jnp.clip(NA_ROWS * grp - NA_KH // 2, 0, n_rows - NA_WIN)
        start = pl.multiple_of(win0 * GRID_W, GRID_W)
        kw = k_ref[pl.ds(start, NA_WIN * GRID_W), :].astype(BF16)
        vw = v_ref[pl.ds(start, NA_WIN * GRID_W), :].astype(BF16)
        q2 = q_ref[j * q_rows:(j + 1) * q_rows, :] * (NA_HEAD_DIM ** -0.5)
        low, q_heads = _split_pair(q2)
        loc, ctx = [], []
        for hh in range(2):
            s_loc = _dot_nt(q_heads[hh], kw) + bias_ref[hh, pattern]
            s_ctx = _dot_nt(q2[:, hh * NA_HEAD_DIM:(hh + 1) * NA_HEAD_DIM].astype(BF16), kcb[hh])
            (e_loc, e_ctx), inv = _softmax_parts([s_loc, s_ctx])
            loc.append(_dot(e_loc.astype(BF16), vw) * inv)
            ctx.append(_dot(e_ctx.astype(BF16), vcb[hh]) * inv)
        o_ref[j * q_rows:(j + 1) * q_rows, :] = (jnp.where(low, loc[0], loc[1])
                                                 + jnp.concatenate(ctx, axis=-1)).astype(o_ref.dtype)


def _mixout_kernel(x_ref, u_ref, mod_ref, cb_ref, cc_ref, ch_ref, ccp_ref, chp_ref, ccn_ref, chn_ref,
                   retc_ref, retl_ref, nac_ref, nal_ref, cw_ref, cbias_ref, wm_ref, bm_ref, wb_ref, wo_ref,
                   o_ref, mb_ref, *, tm, n_ctx, seq, dec_seq):
    i = pl.program_id(0)
    is_ctx = i * tm < n_ctx
    ub = u_ref[...]

    z = cc_ref[...] * ch_ref[...]
    z_before = (ccp_ref[...] * chp_ref[...])[HALO - 1:HALO, :]
    z_after = (ccn_ref[...] * chn_ref[...])[0:1, :]
    row = lax.broadcasted_iota(jnp.int32, z.shape, 0)
    last_pos = jnp.where(is_ctx, seq - 1, dec_seq - 1)
    pos = (i * tm - jnp.where(is_ctx, 0, n_ctx) + row) & last_pos
    z_prev = jnp.where(row == 0, z_before, pltpu.roll(z, 1, 0))
    z_prev = jnp.where(pos == 0, 0.0, z_prev)
    z_next = jnp.where(row == tm - 1, z_after, pltpu.roll(z, tm - 1, 0))
    z_next = jnp.where(pos == last_pos, 0.0, z_next)
    y = z_prev * cw_ref[0:1, :] + z * cw_ref[1:2, :] + z_next * cw_ref[2:3, :] + cbias_ref[...]
    conv_out = cb_ref[...] * y

    ret = jnp.where(is_ctx, retc_ref[...], retl_ref[...])
    na = jnp.where(is_ctx, nac_ref[...], nal_ref[...])
    branches = (ret, conv_out.astype(BF16), na)
    for c in range(D_MODEL // FF_CHUNK):
        merged = None
        for b in range(N_BRANCH):
            sl = slice(b * D_MODEL + c * FF_CHUNK, b * D_MODEL + (c + 1) * FF_CHUNK)
            gate = jax.nn.sigmoid(_dot(ub, wm_ref[:, sl]) + bm_ref[:, sl])
            term = gate * _dot(branches[b], wb_ref[b, :, c * FF_CHUNK:(c + 1) * FF_CHUNK])
            merged = term if merged is None else merged + term
        mb_ref[:, c * FF_CHUNK:(c + 1) * FF_CHUNK] = merged.astype(BF16)
    o_ref[...] = x_ref[...] + mod_ref[5:6, :] * _dot(mb_ref[...], wo_ref[...])


def _final_kernel(x_ref, g_ref, o_ref):
    x = x_ref[...]
    ms = jnp.mean(x * x, axis=-1, keepdims=True)
    o_ref[...] = (x * lax.rsqrt(ms + EPS)) * g_ref[...]


def _rope_tables(length, dim):
    t = jnp.arange(length)
    half = dim // 2
    quarter = half // 2
    inv_freq = ROPE_BASE ** (-jnp.arange(quarter, dtype=F32) * 2.0 / half)

    def tables(pos):
        ang = pos.astype(F32)[:, None] * inv_freq[None, :]
        cos, sin = jnp.cos(ang), jnp.sin(ang)
        return jnp.concatenate([cos, cos], axis=-1), jnp.concatenate([-sin, sin], axis=-1)

    cr, sr = tables(t // GRID_W)
    cc, sc = tables(t % GRID_W)
    return jnp.concatenate([cr, cc], axis=-1), jnp.concatenate([sr, sc], axis=-1)


def _decay_tables(decay_logit):
    c = RET_CHUNK
    log_g = jax.nn.log_sigmoid(decay_logit.astype(F32))
    idx = jnp.arange(c, dtype=F32)
    diff = idx[:, None] - idx[None, :]
    out = []
    for d in range(2):
        lg = log_g[d]
        decay_in = jnp.where(diff >= 0, jnp.exp(lg[:, None, None] * jnp.maximum(diff, 0.0)), 0.0)
        xi = jnp.exp(lg[:, None] * (idx + 1.0))
        zeta = jnp.exp(lg[:, None] * (c - 1.0 - idx))
        g_chunk = jnp.exp(lg * c)
        if d == 1:
            decay_in = jnp.swapaxes(decay_in, 1, 2)
            xi = xi[:, ::-1]
            zeta = zeta[:, ::-1]
        full = (c, c)
        out.append(jnp.stack([decay_in,
                              jnp.broadcast_to(xi[:, :, None], (N_RET_HEADS,) + full),
                              jnp.broadcast_to(zeta[:, :, None], (N_RET_HEADS,) + full),
                              jnp.broadcast_to(g_chunk[:, None, None], (N_RET_HEADS,) + full)], axis=1))
    return jnp.stack(out, axis=0)


def _na_bias_table(rpb):
    h, n_dr, n_dc = rpb.shape
    w = GRID_W
    lead = w - NA_KW
    g = jnp.pad(rpb.astype(F32), ((0, 0), (0, 0), (lead, 2 * w - lead - n_dc)), constant_values=NEG_INF)
    flat = jnp.broadcast_to(g[:, :, None, :], (h, n_dr, w, 2 * w)).reshape(h, n_dr, 2 * w * w)
    tiles = flat[:, :, w - 1:w - 1 + w * (2 * w - 1)].reshape(h, n_dr, w, 2 * w - 1)[:, :, :, :w]
    cq = np.arange(w)
    c0 = np.clip(cq - NA_KW // 2, 0, w - NA_KW)
    col_ok = (cq[None, :] >= c0[:, None]) & (cq[None, :] < c0[:, None] + NA_KW)
    tiles = jnp.where(col_ok[None, None], tiles, NEG_INF)
    tiles = jnp.concatenate([tiles, jnp.full((h, 1, w, w), NEG_INF, F32)], axis=1)
    rr = np.arange(NA_ROWS)
    kr = np.arange(NA_WIN)
    rel_r = np.stack([rr, rr + NA_KH // 2, rr + NA_WIN - NA_ROWS])
    rel_r0 = np.stack([np.zeros_like(rr), rr, np.full_like(rr, NA_WIN - NA_KH)])
    row_ok = (kr[None, None, :] >= rel_r0[:, :, None]) & (kr[None, None, :] < rel_r0[:, :, None] + NA_KH)
    dr = np.where(row_ok, kr[None, None, :] - rel_r[:, :, None] + (NA_KH - 1), n_dr)
    bias = jnp.stack([jnp.stack([jnp.concatenate([tiles[:, int(dr[p, r, k])] for k in range(NA_WIN)], axis=-1)
                                 for r in range(NA_ROWS)], axis=1) for p in range(3)], axis=1)
    return bias.reshape(h, 3, NA_ROWS * w, NA_WIN * w)


def kernel(x_prompt, x_sample, c, state_ret, cache_na_k, cache_na_v, c_ctx, norm_g, w_mod, b_mod, ffn_w1, ffn_w2,
           w_in, ret_decay_logit, conv_w, conv_b, na_rpb, w_branch, w_merge, b_merge, w_out, final_g):
    batch, seq, d = x_prompt.shape
    dec_batch, dec_seq, _ = x_sample.shape
    depth = norm_g.shape[0]
    past_len = cache_na_k.shape[3]
    n_ctx = batch * seq
    n_lat = dec_batch * dec_seq
    n_tok = n_ctx + n_lat
    tm = TOKEN_BLOCK
    tf = FFN_BLOCK
    n_rows = dec_seq // GRID_W
    assert d == D_MODEL and 1 + dec_batch <= COND_ROWS
    assert n_ctx % tm == 0 and dec_seq % tm == 0 and tm % seq == 0 and n_ctx % tf == 0 and dec_seq % tf == 0
    assert seq & (seq - 1) == 0 and dec_seq & (dec_seq - 1) == 0
    assert seq % RET_CHUNK == 0 and dec_seq % RET_CHUNK == 0 and n_ctx % n_lat == 0
    assert n_rows % (NA_ROWS * NA_SUB) == 0 and n_rows >= NA_WIN and D_FF % FF_CHUNK == 0
    n_blocks = n_tok // tm
    ctx_blocks = n_ctx // tm

    x = jnp.concatenate([x_prompt.reshape(n_ctx, d), x_sample.reshape(n_lat, d)], axis=0)
    cond = jnp.zeros((COND_ROWS, d), F32).at[0].set(c_ctx).at[1:1 + dec_batch].set(c)
    mod = _mod_call(cond, w_mod, b_mod).reshape(depth, COND_ROWS, N_MOD, d)

    w1_b = ffn_w1.astype(BF16)
    w2_b = ffn_w2.astype(BF16)
    w_in_b = w_in.astype(BF16)
    w_merge_b = w_merge.astype(BF16)
    w_branch_b = w_branch.astype(BF16)
    w_out_b = w_out.astype(BF16)
    b_merge3 = b_merge.reshape(depth, 1, N_BRANCH * d)
    conv_b3 = conv_b.reshape(depth, 1, BRANCH_W)
    rope_cos, rope_sin = _rope_tables(dec_seq, RET_DK)
    rope_cos = jnp.concatenate([jnp.ones((tm, RET_DK), F32), rope_cos], axis=0)
    rope_sin = jnp.concatenate([jnp.zeros((tm, RET_DK), F32), rope_sin], axis=0)
    n_chunks_lat = dec_seq // RET_CHUNK
    n_chunks_ctx = seq // RET_CHUNK
    na_steps = n_rows // (NA_ROWS * NA_SUB)
    proj_w = N_SPLIT * BRANCH_W

    def mod_spec(l, rows=tm):
        def index(i):
            row = i * rows
            return (l, jnp.where(row < n_ctx, 0, 1 + (row - n_ctx) // dec_seq), 0, 0)
        return pl.BlockSpec((None, None, N_MOD, d), index)

    def g_spec(l):
        return pl.BlockSpec((None, 3, d), lambda i: (l, 0, 0))

    x_spec = pl.BlockSpec((tm, d), lambda i: (i, 0))
    x_shape = jax.ShapeDtypeStruct((n_tok, d), F32)
    hbm_spec = pl.BlockSpec(memory_space=pl.ANY)

    def ffn(xv, l, s):
        rows_spec = pl.BlockSpec((tf, d), lambda i: (i, 0))
        return pl.pallas_call(
            functools.partial(_ffn_kernel, gi=2 * s, mi=6 * s),
            grid=(n_tok // tf,),
            in_specs=[rows_spec, mod_spec(l, tf), g_spec(l),
                      _resident((None, None, d, 2 * D_FF), lambda i: (l, s, 0, 0)),
                      _resident((None, None, D_FF, d), lambda i: (l, s, 0, 0))],
            out_specs=rows_spec,
            out_shape=x_shape,
            scratch_shapes=[pltpu.VMEM((tf, d), BF16), pltpu.VMEM((tf, D_FF), BF16)],
            compiler_params=_params(1),
            name="ffn",
        )(xv, mod, norm_g, w1_b, w2_b)

    def mixin(xv, l):
        def table_block(i):
            row = i * tm
            return (jnp.where(row < n_ctx, 0, 1 + ((row - n_ctx) % dec_seq) // tm), 0)

        tab = pl.BlockSpec((tm, RET_DK), table_block)
        return pl.pallas_call(
            _mixin_kernel,
            grid=(n_blocks,),
            in_specs=[x_spec, mod_spec(l), g_spec(l), _resident((None, d, proj_w), lambda i: (l, 0, 0)), tab, tab],
            out_specs=[pl.BlockSpec((tm, proj_w), lambda i: (i, 0)), x_spec],
            out_shape=[jax.ShapeDtypeStruct((n_tok, proj_w), F32), jax.ShapeDtypeStruct((n_tok, d), BF16)],
            compiler_params=_params(1),
            name="mixer_in",
        )(xv, mod, norm_g, w_in_b, rope_cos, rope_sin)

    def ret_ctx(proj, dec, states, l):
        col = lambda k: pl.BlockSpec((seq, BRANCH_W), lambda b: (b, k))
        return pl.pallas_call(
            functools.partial(_ret_ctx_kernel, n_chunks=n_chunks_ctx),
            grid=(batch,),
            in_specs=[col(0), col(1), col(2), col(3),
                      _resident((None, 2, N_RET_HEADS, 4, RET_CHUNK, RET_CHUNK), lambda b: (l, 0, 0, 0, 0, 0)),
                      hbm_spec],
            out_specs=[pl.BlockSpec((seq, BRANCH_W), lambda b: (b, 0)),
                       pl.BlockSpec((None, None, 2, N_RET_HEADS, RET_DK, RET_DV), lambda b: (b, l, 0, 0, 0, 0))],
            out_shape=[jax.ShapeDtypeStruct((n_ctx, BRANCH_W), BF16),
                       jax.ShapeDtypeStruct(states.shape, F32)],
            input_output_aliases={5: 1},
            compiler_params=_params(1),
            name="retention_ctx",
        )(proj, proj, proj, proj, dec, states)

    def ret_lat(proj, dec, l):
        proj4 = proj.reshape(n_tok // n_lat, dec_batch, dec_seq, proj_w)
        lat = n_ctx // n_lat

        def chunk_of(p, ci):
            return jnp.where(p == 0, ci, n_chunks_lat - 1 - ci)

        col = lambda k: pl.BlockSpec((None, dec_batch, RET_CHUNK, BRANCH_W),
                                     lambda p, ci: (lat, 0, chunk_of(p, ci), k))
        return pl.pallas_call(
            functools.partial(_ret_lat_kernel, n_chunks=n_chunks_lat, n_seqs=dec_batch),
            grid=(2, n_chunks_lat),
            in_specs=[col(0), col(1), col(2), col(3),
                      pl.BlockSpec((dec_batch, None, None, N_RET_HEADS, RET_DK, RET_DV),
                                   lambda p, ci: (0, l, p, 0, 0, 0)),
                      pl.BlockSpec((None, None, N_RET_HEADS, 4, RET_CHUNK, RET_CHUNK),
                                   lambda p, ci: (l, p, 0, 0, 0, 0))],
            out_specs=pl.BlockSpec((dec_batch, RET_CHUNK, BRANCH_W),
                                   lambda p, ci: (0, n_chunks_lat - 1 - jnp.where(p == 0, 0, ci), 0)),
            out_shape=jax.ShapeDtypeStruct((dec_batch, dec_seq, BRANCH_W), BF16),
            scratch_shapes=[pltpu.VMEM((dec_batch, N_RET_HEADS, RET_DK, RET_DV), F32),
                            pltpu.VMEM((dec_batch, dec_seq, BRANCH_W), F32)],
            compiler_params=_params(2),
            name="retention_lat",
        )(proj4, proj4, proj4, proj4, state_ret, dec).reshape(n_lat, BRANCH_W)

    def na_ctx(proj, ck, cv, l):
        col = lambda k: pl.BlockSpec((seq, BRANCH_W), lambda b: (b, k))
        cache_spec = pl.BlockSpec((None, None, N_NA_HEADS, seq, NA_HEAD_DIM), lambda b: (b, l, 0, 0, 0))
        cache_shape = jax.ShapeDtypeStruct(ck.shape, F32)
        return pl.pallas_call(
            _ctx_attn_kernel,
            grid=(batch,),
            in_specs=[col(7), col(8), col(9), hbm_spec, hbm_spec],
            out_specs=[pl.BlockSpec((seq, BRANCH_W), lambda b: (b, 0)), cache_spec, cache_spec],
            out_shape=[jax.ShapeDtypeStruct((n_ctx, BRANCH_W), BF16), cache_shape, cache_shape],
            input_output_aliases={3: 1, 4: 2},
            compiler_params=_params(1),
            name="attention_ctx",
        )(proj, proj, proj, ck, cv)

    def na_lat(proj, bias, l):
        q_rows = NA_SUB * NA_ROWS * GRID_W
        pairs = N_NA_HEADS // 2
        q_base = n_ctx // q_rows
        kv_base = n_ctx // dec_seq
        cache_spec = pl.BlockSpec((None, None, 2, past_len, NA_HEAD_DIM), lambda b, hp, i: (b, l, hp, 0, 0))
        return pl.pallas_call(
            functools.partial(_na_kernel, n_rows=n_rows),
            grid=(dec_batch, pairs, na_steps),
            in_specs=[pl.BlockSpec((q_rows, LANES), lambda b, hp, i: (q_base + b * na_steps + i, 7 * pairs + hp)),
                      pl.BlockSpec((dec_seq, LANES), lambda b, hp, i: (kv_base + b, 8 * pairs + hp)),
                      pl.BlockSpec((dec_seq, LANES), lambda b, hp, i: (kv_base + b, 9 * pairs + hp)),
                      cache_spec, cache_spec,
                      pl.BlockSpec((None, 2, 3, NA_ROWS * GRID_W, NA_WIN * GRID_W),
                                   lambda b, hp, i: (l, hp, 0, 0, 0))],
            out_specs=pl.BlockSpec((q_rows, LANES), lambda b, hp, i: (b * na_steps + i, hp)),
            out_shape=jax.ShapeDtypeStruct((n_lat, BRANCH_W), BF16),
            compiler_params=_params(3),
            name="attention_lat",
        )(proj, proj, proj, cache_na_k, cache_na_v, bias)

    def mixout(xv, ub, proj, ret_c, ret_l, na_c, na_l, l):
        col = lambda k: pl.BlockSpec((tm, BRANCH_W), lambda i: (i, k))
        before = lambda k: pl.BlockSpec((HALO, BRANCH_W), lambda i: (jnp.maximum(i * (tm // HALO) - 1, 0), k))
        after = lambda k: pl.BlockSpec((HALO, BRANCH_W),
                                       lambda i: (jnp.minimum((i + 1) * (tm // HALO), n_tok // HALO - 1), k))
        ctx_rows = pl.BlockSpec((tm, BRANCH_W), lambda i: (jnp.minimum(i, ctx_blocks - 1), 0))
        lat_rows = pl.BlockSpec((tm, BRANCH_W), lambda i: (jnp.maximum(i - ctx_blocks, 0), 0))
        return pl.pallas_call(
            functools.partial(_mixout_kernel, tm=tm, n_ctx=n_ctx, seq=seq, dec_seq=dec_seq),
            grid=(n_blocks,),
            in_specs=[x_spec, x_spec, mod_spec(l), col(4), col(5), col(6), before(5), before(6), after(5),
                      after(6), ctx_rows, lat_rows, ctx_rows, lat_rows,
                      pl.BlockSpec((None, CONV_K, BRANCH_W), lambda i: (l, 0, 0)),
                      pl.BlockSpec((None, 1, BRANCH_W), lambda i: (l, 0, 0)),
                      _resident((None, d, N_BRANCH * d), lambda i: (l, 0, 0)),
                      pl.BlockSpec((None, 1, N_BRANCH * d), lambda i: (l, 0, 0)),
                      _resident((None, N_BRANCH, BRANCH_W, d), lambda i: (l, 0, 0, 0)),
                      _resident((None, d, d), lambda i: (l, 0, 0))],
            out_specs=x_spec,
            out_shape=x_shape,
            scratch_shapes=[pltpu.VMEM((tm, d), BF16)],
            compiler_params=_params(1),
            name="mixer_out",
        )(xv, ub, mod, proj, proj, proj, proj, proj, proj, proj, ret_c, ret_l, na_c, na_l, conv_w, conv_b3,
          w_merge_b, b_merge3, w_branch_b, w_out_b)

    def final(xv, first_block, n_rows_out):
        return pl.pallas_call(
            _final_kernel,
            grid=(n_rows_out // tm,),
            in_specs=[pl.BlockSpec((tm, d), lambda i: (first_block + i, 0)),
                      pl.BlockSpec((1, d), lambda i: (0, 0))],
            out_specs=pl.BlockSpec((tm, d), lambda i: (i, 0)),
            out_shape=jax.ShapeDtypeStruct((n_rows_out, d), F32),
            compiler_params=_params(1),
            name="final_norm",
        )(xv, final_g.reshape(1, d))

    states = jnp.zeros((batch, depth, 2, N_RET_HEADS, RET_DK, RET_DV), F32)
    cache_k = jnp.zeros((batch, depth, N_NA_HEADS, seq, NA_HEAD_DIM), F32)
    cache_v = jnp.zeros((batch, depth, N_NA_HEADS, seq, NA_HEAD_DIM), F32)
    dec_all = jax.vmap(_decay_tables)(ret_decay_logit)
    bias_all = _na_bias_table(na_rpb.reshape((depth * N_NA_HEADS,) + na_rpb.shape[2:]))
    bias_all = bias_all.reshape((depth, N_NA_HEADS) + bias_all.shape[1:])
    for l in range(depth):
        x = ffn(x, l, 0)
        proj, ub = mixin(x, l)
        ret_c, states = ret_ctx(proj, dec_all, states, l)
        ret_l = ret_lat(proj, dec_all, l)
        na_c, cache_k, cache_v = na_ctx(proj, cache_k, cache_v, l)
        na_l = na_lat(proj, bias_all, l)
        x = mixout(x, ub, proj, ret_c, ret_l, na_c, na_l, l)
        x = ffn(x, l, 1)

    y_prompt = final(x, 0, n_ctx).reshape(batch, seq, d)
    y_sample = final(x, ctx_blocks, n_lat).reshape(dec_batch, dec_seq, d)
    return (y_prompt, y_sample, states, cache_k, cache_v)
```

```python
import functools

import jax
import jax.numpy as jnp
import numpy as np
from jax import lax
from jax.experimental import pallas as pl
from jax.experimental.pallas import tpu as pltpu

D_MODEL = 1024
GRID_W = 64
N_RET_HEADS = 4
RET_DK = 128
RET_DV = 128
RET_CHUNK = 128
CONV_K = 3
N_NA_HEADS = 8
NA_HEAD_DIM = 64
NA_KH = 8
NA_KW = 16
BRANCH_W = 512
N_BRANCH = 3
N_SPLIT = 10
D_FF = 2816
N_MOD = 9
ROPE_BASE = 10000.0
EPS = 1e-6
NEG_INF = -1e30

BF16 = jnp.bfloat16
F32 = jnp.float32

VMEM_LIMIT_BYTES = 56 * 1024 * 1024
LANES = 128
COND_ROWS = 8
TOKEN_BLOCK = 512
FFN_BLOCK = 1024
FF_CHUNK = 256
NA_ROWS = 4
NA_WIN = NA_ROWS + NA_KH - 1
NA_SUB = 4
HALO = 8


def _params(n_axes):
    return pltpu.CompilerParams(dimension_semantics=("arbitrary",) * n_axes,
                                vmem_limit_bytes=VMEM_LIMIT_BYTES)


def _resident(block_shape, index_map):
    return pl.BlockSpec(block_shape, index_map, pipeline_mode=pl.Buffered(1))


def _dot(a, b):
    return jnp.dot(a, b, preferred_element_type=F32)


def _dot_nt(a, b):
    return lax.dot_general(a, b, (((1,), (1,)), ((), ())), preferred_element_type=F32)


def _dot_tn(a, b):
    return lax.dot_general(a, b, (((0,), (0,)), ((), ())), preferred_element_type=F32)


def _norm_mod(x, g_row, scale_row, shift_row):
    ms = jnp.mean(x * x, axis=-1, keepdims=True)
    y = x * lax.rsqrt(ms + EPS)
    return (y * g_row) * (1.0 + scale_row) + shift_row


def _mod_kernel(cond_ref, w_ref, b_ref, o_ref):
    a = jax.nn.silu(cond_ref[...]).astype(BF16)
    o_ref[...] = _dot(a, w_ref[...].astype(BF16)) + b_ref[...]


def _mod_call(cond, w_mod, b_mod):
    depth, d, n = w_mod.shape
    tn = 1024
    return pl.pallas_call(
        _mod_kernel,
        grid=(depth, n // tn),
        in_specs=[pl.BlockSpec((COND_ROWS, d), lambda l, j: (0, 0)),
                  pl.BlockSpec((None, d, tn), lambda l, j: (l, 0, j)),
                  pl.BlockSpec((None, 1, tn), lambda l, j: (l, 0, j))],
        out_specs=pl.BlockSpec((None, COND_ROWS, tn), lambda l, j: (l, 0, j)),
        out_shape=jax.ShapeDtypeStruct((depth, COND_ROWS, n), F32),
        compiler_params=_params(2),
        name="mod_vectors",
    )(cond, w_mod, b_mod.reshape(depth, 1, n))


def _ffn_kernel(x_ref, mod_ref, g_ref, w1_ref, w2_ref, o_ref, hb_ref, t_ref, *, gi, mi):
    h = _norm_mod(x_ref[...], g_ref[gi:gi + 1, :], mod_ref[mi + 1:mi + 2, :], mod_ref[mi:mi + 1, :])
    hb_ref[...] = h.astype(BF16)
    for j in range(D_FF // FF_CHUNK):
        hb = hb_ref[...]
        a = _dot(hb, w1_ref[:, j * FF_CHUNK:(j + 1) * FF_CHUNK])
        b = _dot(hb, w1_ref[:, D_FF + j * FF_CHUNK:D_FF + (j + 1) * FF_CHUNK])
        t_ref[:, j * FF_CHUNK:(j + 1) * FF_CHUNK] = (jax.nn.silu(a) * b).astype(BF16)
    o_ref[...] = x_ref[...] + (0.5 * mod_ref[mi + 2:mi + 3, :]) * _dot(t_ref[...], w2_ref[...])


def _mixin_kernel(x_ref, mod_ref, g_ref, w_ref, cos_ref, sin_ref, o_ref, u_ref):
    u = _norm_mod(x_ref[...], g_ref[1:2, :], mod_ref[4:5, :], mod_ref[3:4, :]).astype(BF16)
    u_ref[...] = u
    cos = cos_ref[...]
    sin = sin_ref[...]
    lane = lax.broadcasted_iota(jnp.int32, cos.shape, 1)
    first = (lane % (RET_DK // 2)) < (RET_DK // 4)
    for k in range(N_SPLIT):
        sl = slice(k * BRANCH_W, (k + 1) * BRANCH_W)
        r = _dot(u, w_ref[:, sl])
        if k < 2:
            heads = []
            for h in range(N_RET_HEADS):
                xh = r[:, h * RET_DK:(h + 1) * RET_DK]
                if k == 0:
                    xh = xh * (RET_DK ** -0.5)
                swapped = jnp.where(first, pltpu.roll(xh, RET_DK - RET_DK // 4, 1), pltpu.roll(xh, RET_DK // 4, 1))
                heads.append(xh * cos + swapped * sin)
            r = jnp.concatenate(heads, axis=-1)
        o_ref[:, sl] = r


def _layer_slab(refs, first_layer):
    if not first_layer:
        return refs
    for r in refs:
        r[1:] = jnp.zeros((r.shape[0] - 1,) + r.shape[1:], r.dtype)
    return [r.at[0] for r in refs]


def _decayed_keys_t(k_tiles, zeta_tiles):
    kz = jnp.concatenate([k * z for k, z in zip(k_tiles, zeta_tiles)], axis=-1)
    kzt = kz.T.astype(BF16)
    return [kzt[i * RET_DK:(i + 1) * RET_DK] for i in range(len(k_tiles))]


def _ret_level(items):
    stage = []
    for q, k, kzt, v, s, dm, xi, gc in items:
        qb = q.astype(BF16)
        vb = v.astype(BF16)
        stage.append((vb, s, dm, xi, gc, _dot_nt(qb, k.astype(BF16)), _dot(qb, s.astype(BF16)), _dot(kzt, vb)))
    out = []
    for vb, s, dm, xi, gc, qk, qs, kv in stage:
        s_new = s * gc + kv
        out.append((_dot((qk * dm).astype(BF16), vb) + qs * xi, s_new))
    return out


def _head_norm_gate(os, gs):
    mus = [jnp.mean(o, axis=-1, keepdims=True) for o in os]
    ds = [o - mu for o, mu in zip(os, mus)]
    vs = [jnp.mean(jnp.square(d), axis=-1, keepdims=True) for d in ds]
    return [(d * lax.rsqrt(v + EPS)) * jax.nn.silu(g) for d, v, g in zip(ds, vs, gs)]


def _ret_ctx_kernel(q_ref, k_ref, v_ref, g_ref, dec_ref, *rest, n_chunks, first_layer):
    st_ref = _layer_slab(rest[-1:], first_layer)[0]
    o_ref = rest[-2]
    c = RET_CHUNK
    heads = [slice(h * RET_DK, (h + 1) * RET_DK) for h in range(N_RET_HEADS)]
    rows = [slice(ci * c, (ci + 1) * c) for ci in range(n_chunks)]
    chains = [(d, h) for d in range(2) for h in range(N_RET_HEADS)]
    state = {dh: jnp.zeros((RET_DK, RET_DV), F32) for dh in chains}
    o_tot = {}
    for t in range(n_chunks):
        cis = {(d, h): (t if d == 0 else n_chunks - 1 - t) for d, h in chains}
        kzt = _decayed_keys_t([k_ref[rows[cis[d, h]], heads[h]] for d, h in chains],
                              [dec_ref[d, h, 2] for d, h in chains])
        items = [(q_ref[rows[cis[d, h]], heads[h]], k_ref[rows[cis[d, h]], heads[h]], kzt_dh,
                  v_ref[rows[cis[d, h]], heads[h]], state[d, h], dec_ref[d, h, 0], dec_ref[d, h, 1], dec_ref[d, h, 3])
                 for (d, h), kzt_dh in zip(chains, kzt)]
        for (d, h), (o, s_new) in zip(chains, _ret_level(items)):
            state[d, h] = s_new
            key = (cis[d, h], h)
            o_tot[key] = o if key not in o_tot else o_tot[key] + o
    for d, h in chains:
        st_ref[d, h] = state[d, h]
    tiles = [(ci, h) for ci in range(n_chunks) for h in range(N_RET_HEADS)]
    normed = _head_norm_gate([o_tot[t] for t in tiles], [g_ref[rows[ci], heads[h]] for ci, h in tiles])
    for (ci, h), y in zip(tiles, normed):
        o_ref[rows[ci], heads[h]] = y.astype(o_ref.dtype)


def _ret_lat_kernel(q_ref, k_ref, v_ref, g_ref, s0_ref, dec_ref, o_ref, s_scr, of_scr, *, n_chunks, n_seqs):
    p = pl.program_id(0)
    ci = pl.program_id(1)

    @pl.when(ci == 0)
    def _():
        s_scr[...] = s0_ref[...]

    chunk = jnp.where(p == 0, ci, n_chunks - 1 - ci)
    row0 = pl.multiple_of(chunk * RET_CHUNK, RET_CHUNK)
    heads = [slice(h * RET_DK, (h + 1) * RET_DK) for h in range(N_RET_HEADS)]
    tiles = [(b, h) for b in range(n_seqs) for h in range(N_RET_HEADS)]
    kzt = _decayed_keys_t([k_ref[b, :, heads[h]] for b, h in tiles], [dec_ref[h, 2] for _, h in tiles])
    items = [(q_ref[b, :, heads[h]], k_ref[b, :, heads[h]], kzt_bh, v_ref[b, :, heads[h]], s_scr[b, h],
              dec_ref[h, 0], dec_ref[h, 1], dec_ref[h, 3]) for (b, h), kzt_bh in zip(tiles, kzt)]
    os = []
    for (b, h), (o, s_new) in zip(tiles, _ret_level(items)):
        s_scr[b, h] = s_new
        os.append(o)

    @pl.when(p == 0)
    def _():
        for (b, h), o in zip(tiles, os):
            of_scr[b, pl.ds(row0, RET_CHUNK), heads[h]] = o

    @pl.when(p == 1)
    def _():
        tots = [of_scr[b, pl.ds(row0, RET_CHUNK), heads[h]] + o for (b, h), o in zip(tiles, os)]
        normed = _head_norm_gate(tots, [g_ref[b, :, heads[h]] for b, h in tiles])
        for (b, h), y in zip(tiles, normed):
            o_ref[b, :, heads[h]] = y.astype(o_ref.dtype)


def _softmax_parts(parts):
    m = functools.reduce(jnp.maximum, [jnp.max(s, axis=-1, keepdims=True) for s in parts])
    es = [jnp.exp(s - m) for s in parts]
    den = functools.reduce(lambda a, b: a + b, [jnp.sum(e, axis=-1, keepdims=True) for e in es])
    return es, 1.0 / den


def _split_pair(q2):
    low = lax.broadcasted_iota(jnp.int32, q2.shape, 1) < NA_HEAD_DIM
    return low, (jnp.where(low, q2, 0.0).astype(BF16), jnp.where(low, 0.0, q2).astype(BF16))


def _ctx_attn_kernel(q_ref, k_ref, v_ref, *rest, first_layer):
    o_ref = rest[-3]
    ck_ref, cv_ref = _layer_slab(rest[-2:], first_layer)
    for hp in range(N_NA_HEADS // 2):
        sl = slice(hp * LANES, (hp + 1) * LANES)
        k2 = k_ref[:, sl]
        v2 = v_ref[:, sl]
        for hh in range(2):
            ck_ref[2 * hp + hh] = k2[:, hh * NA_HEAD_DIM:(hh + 1) * NA_HEAD_DIM]
            cv_ref[2 * hp + hh] = v2[:, hh * NA_HEAD_DIM:(hh + 1) * NA_HEAD_DIM]
        kb = k2.astype(BF16)
        vb = v2.astype(BF16)
        low, q_heads = _split_pair(q_ref[:, sl] * (NA_HEAD_DIM ** -0.5))
        outs = []
        for qh in q_heads:
            (e,), inv = _softmax_parts([_dot_nt(qh, kb)])
            outs.append(_dot(e.astype(BF16), vb) * inv)
        o_ref[:, sl] = jnp.where(low, outs[0], outs[1]).astype(o_ref.dtype)


def _na_kernel(q_ref, k_ref, v_ref, kc_ref, vc_ref, tiles_ref, o_ref, bias_ref, *, n_rows):
    i = pl.program_id(2)
    q_rows = NA_ROWS * GRID_W
    n_groups = n_rows // NA_ROWS

    @pl.when(i == 0)
    def _():
        masked = jnp.full((GRID_W, GRID_W), NEG_INF, F32)
        offsets = _na_row_offsets()
        for hh in range(2):
            for p in range(3):
                for r in range(NA_ROWS):
                    row = [masked if offsets[p, r, k] < 0 else tiles_ref[hh, int(offsets[p, r, k])]
                           for k in range(NA_WIN)]
                    bias_ref[hh, p, r * GRID_W:(r + 1) * GRID_W, :] = jnp.concatenate(row, axis=-1)

    kcb = [kc_ref[hh].astype(BF16) for hh in range(2)]
    vcb = [vc_ref[hh].astype(BF16) for hh in range(2)]
    for j in range(NA_SUB):
        grp = i * NA_SUB + j
        pattern = jnp.where(grp == 0, 0, jnp.where(grp == n_groups - 1, 2, 1))
        win0 = jnp.clip(NA_ROWS * grp - NA_KH // 2, 0, n_rows - NA_WIN)
        start = pl.multiple_of(win0 * GRID_W, GRID_W)
        kw = k_ref[pl.ds(start, NA_WIN * GRID_W), :].astype(BF16)
        vw = v_ref[pl.ds(start, NA_WIN * GRID_W), :].astype(BF16)
        q2 = q_ref[j * q_rows:(j + 1) * q_rows, :] * (NA_HEAD_DIM ** -0.5)
        low, q_heads = _split_pair(q2)
        loc, ctx = [], []
        for hh in range(2):
            s_loc = _dot_nt(q_heads[hh], kw) + bias_ref[hh, pattern]
            s_ctx = _dot_nt(q2[:, hh * NA_HEAD_DIM:(hh + 1) * NA_HEAD_DIM].astype(BF16), kcb[hh])
            (e_loc, e_ctx), inv = _softmax_parts([s_loc, s_ctx])
            loc.append(_dot(e_loc.astype(BF16), vw) * inv)
            ctx.append(_dot(e_ctx.astype(BF16), vcb[hh]) * inv)
        o_ref[j * q_rows:(j + 1) * q_rows, :] = (jnp.where(low, loc[0], loc[1])
                                                 + jnp.concatenate(ctx, axis=-1)).astype(o_ref.dtype)


def _mixout_kernel(x_ref, u_ref, mod_ref, cb_ref, cc_ref, ch_ref, ccp_ref, chp_ref, ccn_ref, chn_ref,
                   retc_ref, retl_ref, nac_ref, nal_ref, cw_ref, cbias_ref, wm_ref, bm_ref, wb_ref, wo_ref,
                   o_ref, mb_ref, *, tm, n_ctx, seq, dec_seq):
    i = pl.program_id(0)
    is_ctx = i * tm < n_ctx
    ub = u_ref[...]

    z = cc_ref[...] * ch_ref[...]
    z_before = (ccp_ref[...] * chp_ref[...])[HALO - 1:HALO, :]
    z_after = (ccn_ref[...] * chn_ref[...])[0:1, :]
    row = lax.broadcasted_iota(jnp.int32, z.shape, 0)
    last_pos = jnp.where(is_ctx, seq - 1, dec_seq - 1)
    pos = (i * tm - jnp.where(is_ctx, 0, n_ctx) + row) & last_pos
    z_prev = jnp.where(row == 0, z_before, pltpu.roll(z, 1, 0))
    z_prev = jnp.where(pos == 0, 0.0, z_prev)
    z_next = jnp.where(row == tm - 1, z_after, pltpu.roll(z, tm - 1, 0))
    z_next = jnp.where(pos == last_pos, 0.0, z_next)
    y = z_prev * cw_ref[0:1, :] + z * cw_ref[1:2, :] + z_next * cw_ref[2:3, :] + cbias_ref[...]
    conv_out = cb_ref[...] * y

    ret = jnp.where(is_ctx, retc_ref[...], retl_ref[...])
    na = jnp.where(is_ctx, nac_ref[...], nal_ref[...])
    branches = (ret, conv_out.astype(BF16), na)
    for c in range(D_MODEL // FF_CHUNK):
        merged = None
        for b in range(N_BRANCH):
            sl = slice(b * D_MODEL + c * FF_CHUNK, b * D_MODEL + (c + 1) * FF_CHUNK)
            gate = jax.nn.sigmoid(_dot(ub, wm_ref[:, sl]) + bm_ref[:, sl])
            term = gate * _dot(branches[b], wb_ref[b, :, c * FF_CHUNK:(c + 1) * FF_CHUNK])
            merged = term if merged is None else merged + term
        mb_ref[:, c * FF_CHUNK:(c + 1) * FF_CHUNK] = merged.astype(BF16)
    o_ref[...] = x_ref[...] + mod_ref[5:6, :] * _dot(mb_ref[...], wo_ref[...])


def _final_kernel(x_ref, g_ref, o_ref):
    x = x_ref[...]
    ms = jnp.mean(x * x, axis=-1, keepdims=True)
    o_ref[...] = (x * lax.rsqrt(ms + EPS)) * g_ref[...]


def _rope_tables(length, dim):
    t = jnp.arange(length)
    half = dim // 2
    quarter = half // 2
    inv_freq = ROPE_BASE ** (-jnp.arange(quarter, dtype=F32) * 2.0 / half)

    def tables(pos):
        ang = pos.astype(F32)[:, None] * inv_freq[None, :]
        cos, sin = jnp.cos(ang), jnp.sin(ang)
        return jnp.concatenate([cos, cos], axis=-1), jnp.concatenate([-sin, sin], axis=-1)

    cr, sr = tables(t // GRID_W)
    cc, sc = tables(t % GRID_W)
    return jnp.concatenate([cr, cc], axis=-1), jnp.concatenate([sr, sc], axis=-1)


def _decay_tables(decay_logit):
    c = RET_CHUNK
    log_g = jax.nn.log_sigmoid(decay_logit.astype(F32))
    idx = jnp.arange(c, dtype=F32)
    diff = idx[:, None] - idx[None, :]
    out = []
    for d in range(2):
        lg = log_g[d]
        decay_in = jnp.where(diff >= 0, jnp.exp(lg[:, None, None] * jnp.maximum(diff, 0.0)), 0.0)
        xi = jnp.exp(lg[:, None] * (idx + 1.0))
        zeta = jnp.exp(lg[:, None] * (c - 1.0 - idx))
        g_chunk = jnp.exp(lg * c)
        if d == 1:
            decay_in = jnp.swapaxes(decay_in, 1, 2)
            xi = xi[:, ::-1]
            zeta = zeta[:, ::-1]
        full = (c, c)
        out.append(jnp.stack([decay_in,
                              jnp.broadcast_to(xi[:, :, None], (N_RET_HEADS,) + full),
                              jnp.broadcast_to(zeta[:, :, None], (N_RET_HEADS,) + full),
                              jnp.broadcast_to(g_chunk[:, None, None], (N_RET_HEADS,) + full)], axis=1))
    return jnp.stack(out, axis=0)


def _na_bias_tiles(rpb):
    h, n_dr, n_dc = rpb.shape
    w = GRID_W
    lead = w - NA_KW
    g = jnp.pad(rpb.astype(F32), ((0, 0), (0, 0), (lead, 2 * w - lead - n_dc)), constant_values=NEG_INF)
    flat = jnp.broadcast_to(g[:, :, None, :], (h, n_dr, w, 2 * w)).reshape(h, n_dr, 2 * w * w)
    tiles = flat[:, :, w - 1:w - 1 + w * (2 * w - 1)].reshape(h, n_dr, w, 2 * w - 1)[:, :, :, :w]
    cq = np.arange(w)
    c0 = np.clip(cq - NA_KW // 2, 0, w - NA_KW)
    col_ok = (cq[None, :] >= c0[:, None]) & (cq[None, :] < c0[:, None] + NA_KW)
    return jnp.where(col_ok[None, None], tiles, NEG_INF)


def _na_row_offsets():
    rr = np.arange(NA_ROWS)
    kr = np.arange(NA_WIN)
    rel_r = np.stack([rr, rr + NA_KH // 2, rr + NA_WIN - NA_ROWS])
    rel_r0 = np.stack([np.zeros_like(rr), rr, np.full_like(rr, NA_WIN - NA_KH)])
    row_ok = (kr[None, None, :] >= rel_r0[:, :, None]) & (kr[None, None, :] < rel_r0[:, :, None] + NA_KH)
    return np.where(row_ok, kr[None, None, :] - rel_r[:, :, None] + (NA_KH - 1), -1)


def kernel(x_prompt, x_sample, c, state_ret, cache_na_k, cache_na_v, c_ctx, norm_g, w_mod, b_mod, ffn_w1, ffn_w2,
           w_in, ret_decay_logit, conv_w, conv_b, na_rpb, w_branch, w_merge, b_merge, w_out, final_g):
    batch, seq, d = x_prompt.shape
    dec_batch, dec_seq, _ = x_sample.shape
    depth = norm_g.shape[0]
    past_len = cache_na_k.shape[3]
    n_ctx = batch * seq
    n_lat = dec_batch * dec_seq
    n_tok = n_ctx + n_lat
    tm = TOKEN_BLOCK
    tf = FFN_BLOCK
    n_rows = dec_seq // GRID_W
    assert d == D_MODEL and 1 + dec_batch <= COND_ROWS
    assert n_ctx % tm == 0 and dec_seq % tm == 0 and tm % seq == 0 and n_ctx % tf == 0 and dec_seq % tf == 0
    assert seq & (seq - 1) == 0 and dec_seq & (dec_seq - 1) == 0
    assert seq % RET_CHUNK == 0 and dec_seq % RET_CHUNK == 0 and n_ctx % n_lat == 0
    assert n_rows % (NA_ROWS * NA_SUB) == 0 and n_rows >= NA_WIN and D_FF % FF_CHUNK == 0
    n_blocks = n_tok // tm
    ctx_blocks = n_ctx // tm

    x = jnp.concatenate([x_prompt.reshape(n_ctx, d), x_sample.reshape(n_lat, d)], axis=0)
    cond = jnp.zeros((COND_ROWS, d), F32).at[0].set(c_ctx).at[1:1 + dec_batch].set(c)
    mod = _mod_call(cond, w_mod, b_mod).reshape(depth, COND_ROWS, N_MOD, d)

    w1_b = ffn_w1.astype(BF16)
    w2_b = ffn_w2.astype(BF16)
    w_in_b = w_in.astype(BF16)
    w_merge_b = w_merge.astype(BF16)
    w_branch_b = w_branch.astype(BF16)
    w_out_b = w_out.astype(BF16)
    b_merge3 = b_merge.reshape(depth, 1, N_BRANCH * d)
    conv_b3 = conv_b.reshape(depth, 1, BRANCH_W)
    rope_cos, rope_sin = _rope_tables(dec_seq, RET_DK)
    rope_cos = jnp.concatenate([jnp.ones((tm, RET_DK), F32), rope_cos], axis=0)
    rope_sin = jnp.concatenate([jnp.zeros((tm, RET_DK), F32), rope_sin], axis=0)
    n_chunks_lat = dec_seq // RET_CHUNK
    n_chunks_ctx = seq // RET_CHUNK
    na_steps = n_rows // (NA_ROWS * NA_SUB)
    proj_w = N_SPLIT * BRANCH_W

    def mod_spec(l, rows=tm):
        def index(i):
            row = i * rows
            return (l, jnp.where(row < n_ctx, 0, 1 + (row - n_ctx) // dec_seq), 0, 0)
        return pl.BlockSpec((None, None, N_MOD, d), index)

    def g_spec(l):
        return pl.BlockSpec((None, 3, d), lambda i: (l, 0, 0))

    x_spec = pl.BlockSpec((tm, d), lambda i: (i, 0))
    x_shape = jax.ShapeDtypeStruct((n_tok, d), F32)
    hbm_spec = pl.BlockSpec(memory_space=pl.ANY)

    def ffn(xv, l, s):
        rows_spec = pl.BlockSpec((tf, d), lambda i: (i, 0))
        return pl.pallas_call(
            functools.partial(_ffn_kernel, gi=2 * s, mi=6 * s),
            grid=(n_tok // tf,),
            in_specs=[rows_spec, mod_spec(l, tf), g_spec(l),
                      _resident((None, None, d, 2 * D_FF), lambda i: (l, s, 0, 0)),
                      _resident((None, None, D_FF, d), lambda i: (l, s, 0, 0))],
            out_specs=rows_spec,
            out_shape=x_shape,
            scratch_shapes=[pltpu.VMEM((tf, d), BF16), pltpu.VMEM((tf, D_FF), BF16)],
            compiler_params=_params(1),
            name="ffn",
        )(xv, mod, norm_g, w1_b, w2_b)

    def mixin(xv, l):
        def table_block(i):
            row = i * tm
            return (jnp.where(row < n_ctx, 0, 1 + ((row - n_ctx) % dec_seq) // tm), 0)

        tab = pl.BlockSpec((tm, RET_DK), table_block)
        return pl.pallas_call(
            _mixin_kernel,
            grid=(n_blocks,),
            in_specs=[x_spec, mod_spec(l), g_spec(l), _resident((None, d, proj_w), lambda i: (l, 0, 0)), tab, tab],
            out_specs=[pl.BlockSpec((tm, proj_w), lambda i: (i, 0)), x_spec],
            out_shape=[jax.ShapeDtypeStruct((n_tok, proj_w), F32), jax.ShapeDtypeStruct((n_tok, d), BF16)],
            compiler_params=_params(1),
            name="mixer_in",
        )(xv, mod, norm_g, w_in_b, rope_cos, rope_sin)

    def stacked(shape, l):
        zeros = (0,) * len(shape)
        if l == 0:
            return pl.BlockSpec((None, depth) + shape, lambda b: (b, 0) + zeros)
        return pl.BlockSpec((None, None) + shape, lambda b: (b, l) + zeros)

    def ret_ctx(proj, dec, states, l):
        col = lambda k: pl.BlockSpec((seq, BRANCH_W), lambda b: (b, k))
        state_shape = (2, N_RET_HEADS, RET_DK, RET_DV)
        carried = () if l == 0 else (states,)
        return pl.pallas_call(
            functools.partial(_ret_ctx_kernel, n_chunks=n_chunks_ctx, first_layer=l == 0),
            grid=(batch,),
            in_specs=[col(0), col(1), col(2), col(3),
                      _resident((None, 2, N_RET_HEADS, 4, RET_CHUNK, RET_CHUNK), lambda b: (l, 0, 0, 0, 0, 0))]
            + [hbm_spec] * len(carried),
            out_specs=[pl.BlockSpec((seq, BRANCH_W), lambda b: (b, 0)), stacked(state_shape, l)],
            out_shape=[jax.ShapeDtypeStruct((n_ctx, BRANCH_W), BF16),
                       jax.ShapeDtypeStruct((batch, depth) + state_shape, F32)],
            input_output_aliases={5: 1} if carried else {},
            compiler_params=_params(1),
            name="retention_ctx",
        )(proj, proj, proj, proj, dec, *carried)

    def ret_lat(proj, dec, l):
        proj4 = proj.reshape(n_tok // n_lat, dec_batch, dec_seq, proj_w)
        lat = n_ctx // n_lat

        def chunk_of(p, ci):
            return jnp.where(p == 0, ci, n_chunks_lat - 1 - ci)

        col = lambda k: pl.BlockSpec((None, dec_batch, RET_CHUNK, BRANCH_W),
                                     lambda p, ci: (lat, 0, chunk_of(p, ci), k))
        return pl.pallas_call(
            functools.partial(_ret_lat_kernel, n_chunks=n_chunks_lat, n_seqs=dec_batch),
            grid=(2, n_chunks_lat),
            in_specs=[col(0), col(1), col(2), col(3),
                      pl.BlockSpec((dec_batch, None, None, N_RET_HEADS, RET_DK, RET_DV),
                                   lambda p, ci: (0, l, p, 0, 0, 0)),
                      pl.BlockSpec((None, None, N_RET_HEADS, 4, RET_CHUNK, RET_CHUNK),
                                   lambda p, ci: (l, p, 0, 0, 0, 0))],
            out_specs=pl.BlockSpec((dec_batch, RET_CHUNK, BRANCH_W),
                                   lambda p, ci: (0, n_chunks_lat - 1 - jnp.where(p == 0, 0, ci), 0)),
            out_shape=jax.ShapeDtypeStruct((dec_batch, dec_seq, BRANCH_W), BF16),
            scratch_shapes=[pltpu.VMEM((dec_batch, N_RET_HEADS, RET_DK, RET_DV), F32),
                            pltpu.VMEM((dec_batch, dec_seq, BRANCH_W), F32)],
            compiler_params=_params(2),
            name="retention_lat",
        )(proj4, proj4, proj4, proj4, state_ret, dec).reshape(n_lat, BRANCH_W)

    def na_ctx(proj, ck, cv, l):
        col = lambda k: pl.BlockSpec((seq, BRANCH_W), lambda b: (b, k))
        slab = (N_NA_HEADS, seq, NA_HEAD_DIM)
        cache_shape = jax.ShapeDtypeStruct((batch, depth) + slab, F32)
        carried = () if l == 0 else (ck, cv)
        return pl.pallas_call(
            functools.partial(_ctx_attn_kernel, first_layer=l == 0),
            grid=(batch,),
            in_specs=[col(7), col(8), col(9)] + [hbm_spec] * len(carried),
            out_specs=[pl.BlockSpec((seq, BRANCH_W), lambda b: (b, 0)), stacked(slab, l), stacked(slab, l)],
            out_shape=[jax.ShapeDtypeStruct((n_ctx, BRANCH_W), BF16), cache_shape, cache_shape],
            input_output_aliases={3: 1, 4: 2} if carried else {},
            compiler_params=_params(1),
            name="attention_ctx",
        )(proj, proj, proj, *carried)

    def na_lat(proj, tiles, l):
        q_rows = NA_SUB * NA_ROWS * GRID_W
        pairs = N_NA_HEADS // 2
        q_base = n_ctx // q_rows
        kv_base = n_ctx // dec_seq
        cache_spec = pl.BlockSpec((None, None, 2, past_len, NA_HEAD_DIM), lambda b, hp, i: (b, l, hp, 0, 0))
        return pl.pallas_call(
            functools.partial(_na_kernel, n_rows=n_rows),
            grid=(dec_batch, pairs, na_steps),
            in_specs=[pl.BlockSpec((q_rows, LANES), lambda b, hp, i: (q_base + b * na_steps + i, 7 * pairs + hp)),
                      pl.BlockSpec((dec_seq, LANES), lambda b, hp, i: (kv_base + b, 8 * pairs + hp)),
                      pl.BlockSpec((dec_seq, LANES), lambda b, hp, i: (kv_base + b, 9 * pairs + hp)),
                      cache_spec, cache_spec,
                      pl.BlockSpec((None, 2, 2 * NA_KH - 1, GRID_W, GRID_W), lambda b, hp, i: (l, hp, 0, 0, 0))],
            out_specs=pl.BlockSpec((q_rows, LANES), lambda b, hp, i: (b * na_steps + i, hp)),
            out_shape=jax.ShapeDtypeStruct((n_lat, BRANCH_W), BF16),
            scratch_shapes=[pltpu.VMEM((2, 3, NA_ROWS * GRID_W, NA_WIN * GRID_W), F32)],
            compiler_params=_params(3),
            name="attention_lat",
        )(proj, proj, proj, cache_na_k, cache_na_v, tiles)

    def mixout(xv, ub, proj, ret_c, ret_l, na_c, na_l, l):
        col = lambda k: pl.BlockSpec((tm, BRANCH_W), lambda i: (i, k))
        before = lambda k: pl.BlockSpec((HALO, BRANCH_W), lambda i: (jnp.maximum(i * (tm // HALO) - 1, 0), k))
        after = lambda k: pl.BlockSpec((HALO, BRANCH_W),
                                       lambda i: (jnp.minimum((i + 1) * (tm // HALO), n_tok // HALO - 1), k))
        ctx_rows = pl.BlockSpec((tm, BRANCH_W), lambda i: (jnp.minimum(i, ctx_blocks - 1), 0))
        lat_rows = pl.BlockSpec((tm, BRANCH_W), lambda i: (jnp.maximum(i - ctx_blocks, 0), 0))
        return pl.pallas_call(
            functools.partial(_mixout_kernel, tm=tm, n_ctx=n_ctx, seq=seq, dec_seq=dec_seq),
            grid=(n_blocks,),
            in_specs=[x_spec, x_spec, mod_spec(l), col(4), col(5), col(6), before(5), before(6), after(5),
                      after(6), ctx_rows, lat_rows, ctx_rows, lat_rows,
                      pl.BlockSpec((None, CONV_K, BRANCH_W), lambda i: (l, 0, 0)),
                      pl.BlockSpec((None, 1, BRANCH_W), lambda i: (l, 0, 0)),
                      _resident((None, d, N_BRANCH * d), lambda i: (l, 0, 0)),
                      pl.BlockSpec((None, 1, N_BRANCH * d), lambda i: (l, 0, 0)),
                      _resident((None, N_BRANCH, BRANCH_W, d), lambda i: (l, 0, 0, 0)),
                      _resident((None, d, d), lambda i: (l, 0, 0))],
            out_specs=x_spec,
            out_shape=x_shape,
            scratch_shapes=[pltpu.VMEM((tm, d), BF16)],
            compiler_params=_params(1),
            name="mixer_out",
        )(xv, ub, mod, proj, proj, proj, proj, proj, proj, proj, ret_c, ret_l, na_c, na_l, conv_w, conv_b3,
          w_merge_b, b_merge3, w_branch_b, w_out_b)

    def final(xv, first_block, n_rows_out):
        return pl.pallas_call(
            _final_kernel,
            grid=(n_rows_out // tm,),
            in_specs=[pl.BlockSpec((tm, d), lambda i: (first_block + i, 0)),
                      pl.BlockSpec((1, d), lambda i: (0, 0))],
            out_specs=pl.BlockSpec((tm, d), lambda i: (i, 0)),
            out_shape=jax.ShapeDtypeStruct((n_rows_out, d), F32),
            compiler_params=_params(1),
            name="final_norm",
        )(xv, final_g.reshape(1, d))

    dec_all = jax.vmap(_decay_tables)(ret_decay_logit)
    tiles_all = _na_bias_tiles(na_rpb.reshape((depth * N_NA_HEADS,) + na_rpb.shape[2:]))
    tiles_all = tiles_all.reshape((depth, N_NA_HEADS) + tiles_all.shape[1:])
    states = cache_k = cache_v = None
    for l in range(depth):
        x = ffn(x, l, 0)
        proj, ub = mixin(x, l)
        ret_c, states = ret_ctx(proj, dec_all, states, l)
        ret_l = ret_lat(proj, dec_all, l)
        na_c, cache_k, cache_v = na_ctx(proj, cache_k, cache_v, l)
        na_l = na_lat(proj, tiles_all, l)
        x = mixout(x, ub, proj, ret_c, ret_l, na_c, na_l, l)
        x = ffn(x, l, 1)

    y_prompt = final(x, 0, n_ctx).reshape(batch, seq, d)
    y_sample = final(x, ctx_blocks, n_lat).reshape(dec_batch, dec_seq, d)
    return (y_prompt, y_sample, states, cache_k, cache_v)
```

```python
import functools

import jax
import jax.numpy as jnp
import numpy as np
from jax import lax
from jax.experimental import pallas as pl
from jax.experimental.pallas import tpu as pltpu

D_MODEL = 1024
GRID_W = 64
N_RET_HEADS = 4
RET_DK = 128
RET_DV = 128
RET_CHUNK = 128
CONV_K = 3
N_NA_HEADS = 8
NA_HEAD_DIM = 64
NA_KH = 8
NA_KW = 16
BRANCH_W = 512
N_BRANCH = 3
N_SPLIT = 10
D_FF = 2816
N_MOD = 9
ROPE_BASE = 10000.0
EPS = 1e-6
NEG_INF = -1e30
LOG2E = 1.4426950408889634

BF16 = jnp.bfloat16
F32 = jnp.float32

VMEM_LIMIT_BYTES = 56 * 1024 * 1024
LANES = 128
COND_ROWS = 8
TOKEN_BLOCK = 512
FFN_BLOCK = 1024
FF_CHUNK = 256
NA_ROWS = 4
NA_WIN = NA_ROWS + NA_KH - 1
NA_SUB = 4
HALO = 8


def _params(n_axes):
    return pltpu.CompilerParams(dimension_semantics=("arbitrary",) * n_axes,
                                vmem_limit_bytes=VMEM_LIMIT_BYTES)


def _resident(block_shape, index_map):
    return pl.BlockSpec(block_shape, index_map, pipeline_mode=pl.Buffered(1))


def _dot(a, b):
    return jnp.dot(a, b, preferred_element_type=F32)


def _dot_nt(a, b):
    return lax.dot_general(a, b, (((1,), (1,)), ((), ())), preferred_element_type=F32)


def _norm_mod(x, g_row, scale_row, shift_row):
    ms = jnp.mean(x * x, axis=-1, keepdims=True)
    y = x * lax.rsqrt(ms + EPS)
    return (y * g_row) * (1.0 + scale_row) + shift_row


def _mod_kernel(cond_ref, w_ref, b_ref, o_ref):
    a = jax.nn.silu(cond_ref[...]).astype(BF16)
    o_ref[...] = _dot(a, w_ref[...].astype(BF16)) + b_ref[...]


def _mod_call(cond, w_mod, b_mod):
    depth, d, n = w_mod.shape
    tn = 1024
    return pl.pallas_call(
        _mod_kernel,
        grid=(depth, n // tn),
        in_specs=[pl.BlockSpec((COND_ROWS, d), lambda l, j: (0, 0)),
                  pl.BlockSpec((None, d, tn), lambda l, j: (l, 0, j)),
                  pl.BlockSpec((None, 1, tn), lambda l, j: (l, 0, j))],
        out_specs=pl.BlockSpec((None, COND_ROWS, tn), lambda l, j: (l, 0, j)),
        out_shape=jax.ShapeDtypeStruct((depth, COND_ROWS, n), F32),
        compiler_params=_params(2),
        name="mod_vectors",
    )(cond, w_mod, b_mod.reshape(depth, 1, n))


def _ffn_kernel(*refs, gi, mi, ctx_steps, split_in, final):
    refs = list(refs)
    x_refs = [refs.pop(0) for _ in range(2 if split_in else 1)]
    mod_ref, g_ref, w1_ref, w2_ref = (refs.pop(0) for _ in range(4))
    fg_ref = refs.pop(0) if final else None
    o_refs = [refs.pop(0) for _ in range(2 if final else 1)]
    hb_ref, t_ref = refs
    is_ctx = pl.program_id(0) < ctx_steps

    def rows():
        return jnp.where(is_ctx, x_refs[0][...], x_refs[1][...]) if split_in else x_refs[0][...]

    h = _norm_mod(rows(), g_ref[gi:gi + 1, :], mod_ref[mi + 1:mi + 2, :], mod_ref[mi:mi + 1, :])
    hb_ref[...] = h.astype(BF16)
    for j in range(D_FF // FF_CHUNK):
        hb = hb_ref[...]
        a = _dot(hb, w1_ref[:, j * FF_CHUNK:(j + 1) * FF_CHUNK])
        b = _dot(hb, w1_ref[:, D_FF + j * FF_CHUNK:D_FF + (j + 1) * FF_CHUNK])
        t_ref[:, j * FF_CHUNK:(j + 1) * FF_CHUNK] = (jax.nn.silu(a) * b).astype(BF16)
    x_new = rows() + (0.5 * mod_ref[mi + 2:mi + 3, :]) * _dot(t_ref[...], w2_ref[...])
    if not final:
        o_refs[0][...] = x_new
        return
    ms = jnp.mean(x_new * x_new, axis=-1, keepdims=True)
    y = (x_new * lax.rsqrt(ms + EPS)) * fg_ref[...]

    @pl.when(is_ctx)
    def _():
        o_refs[0][...] = y

    @pl.when(jnp.logical_not(is_ctx))
    def _():
        o_refs[1][...] = y


def _mixin_kernel(x_ref, mod_ref, g_ref, w_ref, cos_ref, sin_ref, o_ref, u_ref):
    u = _norm_mod(x_ref[...], g_ref[1:2, :], mod_ref[4:5, :], mod_ref[3:4, :]).astype(BF16)
    u_ref[...] = u
    cos = cos_ref[...]
    sin = sin_ref[...]
    lane = lax.broadcasted_iota(jnp.int32, cos.shape, 1)
    first = (lane % (RET_DK // 2)) < (RET_DK // 4)
    for k in range(N_SPLIT):
        sl = slice(k * BRANCH_W, (k + 1) * BRANCH_W)
        r = _dot(u, w_ref[:, sl])
        if k < 2:
            heads = []
            for h in range(N_RET_HEADS):
                xh = r[:, h * RET_DK:(h + 1) * RET_DK]
                if k == 0:
                    xh = xh * (RET_DK ** -0.5)
                swapped = jnp.where(first, pltpu.roll(xh, RET_DK - RET_DK // 4, 1), pltpu.roll(xh, RET_DK // 4, 1))
                heads.append(xh * cos + swapped * sin)
            r = jnp.concatenate(heads, axis=-1)
        o_ref[:, sl] = r


def _layer_slab(refs, first_layer):
    if not first_layer:
        return refs
    for r in refs:
        if r.shape[0] > 1:
            r[1:] = jnp.zeros((r.shape[0] - 1,) + r.shape[1:], r.dtype)
    return [r.at[0] for r in refs]


def _decayed_keys_t(k_tiles, zeta_tiles):
    kz = jnp.concatenate([k * z for k, z in zip(k_tiles, zeta_tiles)], axis=-1)
    kzt = kz.T.astype(BF16)
    return [kzt[i * RET_DK:(i + 1) * RET_DK] for i in range(len(k_tiles))]


def _ret_level(items):
    stage = []
    for q, k, kzt, v, s, dm, xi, gc in items:
        qb = q.astype(BF16)
        vb = v.astype(BF16)
        stage.append((vb, s, dm, xi, gc, _dot_nt(qb, k.astype(BF16)), _dot(qb, s.astype(BF16)), _dot(kzt, vb)))
    out = []
    for vb, s, dm, xi, gc, qk, qs, kv in stage:
        s_new = s * gc + kv
        out.append((_dot((qk * dm).astype(BF16), vb) + qs * xi, s_new))
    return out


def _head_norm_gate(os, gs):
    mus = [jnp.mean(o, axis=-1, keepdims=True) for o in os]
    ds = [o - mu for o, mu in zip(os, mus)]
    vs = [jnp.mean(jnp.square(d), axis=-1, keepdims=True) for d in ds]
    return [(d * lax.rsqrt(v + EPS)) * jax.nn.silu(g) for d, v, g in zip(ds, vs, gs)]


def _ret_ctx_kernel(q_ref, k_ref, v_ref, g_ref, dec_ref, *rest, n_chunks, first_layer):
    st_ref = _layer_slab(rest[-1:], first_layer)[0]
    o_ref = rest[-2]
    c = RET_CHUNK
    heads = [slice(h * RET_DK, (h + 1) * RET_DK) for h in range(N_RET_HEADS)]
    rows = [slice(ci * c, (ci + 1) * c) for ci in range(n_chunks)]
    chains = [(d, h) for d in range(2) for h in range(N_RET_HEADS)]
    state = {dh: jnp.zeros((RET_DK, RET_DV), F32) for dh in chains}
    o_tot = {}
    for t in range(n_chunks):
        cis = {(d, h): (t if d == 0 else n_chunks - 1 - t) for d, h in chains}
        kzt = _decayed_keys_t([k_ref[rows[cis[d, h]], heads[h]] for d, h in chains],
                              [dec_ref[d, h, 2] for d, h in chains])
        items = [(q_ref[rows[cis[d, h]], heads[h]], k_ref[rows[cis[d, h]], heads[h]], kzt_dh,
                  v_ref[rows[cis[d, h]], heads[h]], state[d, h], dec_ref[d, h, 0], dec_ref[d, h, 1], dec_ref[d, h, 3])
                 for (d, h), kzt_dh in zip(chains, kzt)]
        for (d, h), (o, s_new) in zip(chains, _ret_level(items)):
            state[d, h] = s_new
            key = (cis[d, h], h)
            o_tot[key] = o if key not in o_tot else o_tot[key] + o
    for d, h in chains:
        st_ref[d, h] = state[d, h]
    tiles = [(ci, h) for ci in range(n_chunks) for h in range(N_RET_HEADS)]
    normed = _head_norm_gate([o_tot[t] for t in tiles], [g_ref[rows[ci], heads[h]] for ci, h in tiles])
    for (ci, h), y in zip(tiles, normed):
        o_ref[rows[ci], heads[h]] = y.astype(o_ref.dtype)


def _ret_lat_kernel(q_ref, k_ref, v_ref, g_ref, s0_ref, dec_ref, o_ref, s_scr, of_scr, *, n_chunks, n_seqs):
    p = pl.program_id(0)
    ci = pl.program_id(1)

    @pl.when(ci == 0)
    def _():
        s_scr[...] = s0_ref[...]

    chunk = jnp.where(p == 0, ci, n_chunks - 1 - ci)
    row0 = pl.multiple_of(chunk * RET_CHUNK, RET_CHUNK)
    heads = [slice(h * RET_DK, (h + 1) * RET_DK) for h in range(N_RET_HEADS)]
    tiles = [(b, h) for b in range(n_seqs) for h in range(N_RET_HEADS)]
    kzt = _decayed_keys_t([k_ref[b, :, heads[h]] for b, h in tiles], [dec_ref[h, 2] for _, h in tiles])
    items = [(q_ref[b, :, heads[h]], k_ref[b, :, heads[h]], kzt_bh, v_ref[b, :, heads[h]], s_scr[b, h],
              dec_ref[h, 0], dec_ref[h, 1], dec_ref[h, 3]) for (b, h), kzt_bh in zip(tiles, kzt)]
    os = []
    for (b, h), (o, s_new) in zip(tiles, _ret_level(items)):
        s_scr[b, h] = s_new
        os.append(o)

    @pl.when(p == 0)
    def _():
        for (b, h), o in zip(tiles, os):
            of_scr[b, pl.ds(row0, RET_CHUNK), heads[h]] = o

    @pl.when(p == 1)
    def _():
        tots = [of_scr[b, pl.ds(row0, RET_CHUNK), heads[h]] + o for (b, h), o in zip(tiles, os)]
        normed = _head_norm_gate(tots, [g_ref[b, :, heads[h]] for b, h in tiles])
        for (b, h), y in zip(tiles, normed):
            o_ref[b, :, heads[h]] = y.astype(o_ref.dtype)


def _softmax_parts(parts):
    m = functools.reduce(jnp.maximum, [jnp.max(s, axis=-1, keepdims=True) for s in parts])
    es = [jnp.exp2(s - m) for s in parts]
    den = functools.reduce(lambda a, b: a + b, [jnp.sum(e, axis=-1, keepdims=True) for e in es])
    return es, 1.0 / den


def _split_pair(q2):
    low = lax.broadcasted_iota(jnp.int32, q2.shape, 1) < NA_HEAD_DIM
    return low, (jnp.where(low, q2, 0.0).astype(BF16), jnp.where(low, 0.0, q2).astype(BF16))


def _ctx_attn_kernel(q_ref, k_ref, v_ref, *rest, first_layer):
    o_ref = rest[-3]
    ck_ref, cv_ref = _layer_slab(rest[-2:], first_layer)
    for hp in range(N_NA_HEADS // 2):
        sl = slice(hp * LANES, (hp + 1) * LANES)
        k2 = k_ref[:, sl]
        v2 = v_ref[:, sl]
        for hh in range(2):
            ck_ref[2 * hp + hh] = k2[:, hh * NA_HEAD_DIM:(hh + 1) * NA_HEAD_DIM]
            cv_ref[2 * hp + hh] = v2[:, hh * NA_HEAD_DIM:(hh + 1) * NA_HEAD_DIM]
        kb = k2.astype(BF16)
        vb = v2.astype(BF16)
        low, q_heads = _split_pair(q_ref[:, sl] * (NA_HEAD_DIM ** -0.5 * LOG2E))
        outs = []
        for qh in q_heads:
            (e,), inv = _softmax_parts([_dot_nt(qh, kb)])
            outs.append(_dot(e.astype(BF16), vb) * inv)
        o_ref[:, sl] = jnp.where(low, outs[0], outs[1]).astype(o_ref.dtype)


def _na_kernel(q_ref, k_ref, v_ref, kc_ref, vc_ref, rpb_ref, o_ref, bias_ref, *, n_rows):
    i = pl.program_id(2)
    q_rows = NA_ROWS * GRID_W
    n_groups = n_rows // NA_ROWS

    @pl.when(i == 0)
    def _():
        w = GRID_W
        masked = jnp.full((w, w), NEG_INF, F32)
        qc = lax.broadcasted_iota(jnp.int32, (w, w), 0)
        kc = lax.broadcasted_iota(jnp.int32, (w, w), 1)
        c0 = jnp.clip(qc - NA_KW // 2, 0, w - NA_KW)
        col_ok = (kc >= c0) & (kc < c0 + NA_KW)
        offsets = _na_row_offsets()
        for hh in range(2):
            tiles = {}
            for dr in sorted({int(o) for o in offsets.ravel() if o >= 0}):
                rows = jnp.broadcast_to(rpb_ref[hh, dr:dr + 1, :], (w, 2 * w))
                skewed = pltpu.roll(rows, w + 1, 1, stride=1, stride_axis=0)
                tiles[dr] = jnp.where(col_ok, skewed[:, :w], NEG_INF)
            for p in range(3):
                for r in range(NA_ROWS):
                    row = [masked if offsets[p, r, k] < 0 else tiles[int(offsets[p, r, k])] for k in range(NA_WIN)]
                    bias_ref[hh, p, r * w:(r + 1) * w, :] = jnp.concatenate(row, axis=-1)

    kcb = [kc_ref[hh].astype(BF16) for hh in range(2)]
    vcb = [vc_ref[hh].astype(BF16) for hh in range(2)]
    for j in range(NA_SUB):
        grp = i * NA_SUB + j
        pattern = jnp.where(grp == 0, 0, jnp.where(grp == n_groups - 1, 2, 1))
        win0 = jnp.clip(NA_ROWS * grp - NA_KH // 2, 0, n_rows - NA_WIN)
        start = pl.multiple_of(win0 * GRID_W, GRID_W)
        kw = k_ref[pl.ds(start, NA_WIN * GRID_W), :].astype(BF16)
        vw = v_ref[pl.ds(start, NA_WIN * GRID_W), :].astype(BF16)
        q2 = q_ref[j * q_rows:(j + 1) * q_rows, :] * (NA_HEAD_DIM ** -0.5 * LOG2E)
        low, q_heads = _split_pair(q2)
        loc, ctx = [], []
        for hh in range(2):
            s_loc = _dot_nt(q_heads[hh], kw) + bias_ref[hh, pattern]
            s_ctx = _dot_nt(q2[:, hh * NA_HEAD_DIM:(hh + 1) * NA_HEAD_DIM].astype(BF16), kcb[hh])
            (e_loc, e_ctx), inv = _softmax_parts([s_loc, s_ctx])
            loc.append(_dot(e_loc.astype(BF16), vw) * inv)
            ctx.append(_dot(e_ctx.astype(BF16), vcb[hh]) * inv)
        o_ref[j * q_rows:(j + 1) * q_rows, :] = (jnp.where(low, loc[0], loc[1])
                                                 + jnp.concatenate(ctx, axis=-1)).astype(o_ref.dtype)


def _mixout_kernel(x_ref, u_ref, mod_ref, cb_ref, cc_ref, ch_ref, ccp_ref, chp_ref, ccn_ref, chn_ref,
                   retc_ref, retl_ref, nac_ref, nal_ref, cw_ref, cbias_ref, wm_ref, bm_ref, wb_ref, wo_ref,
                   o_ref, mb_ref, *, tm, n_ctx, seq, dec_seq):
    i = pl.program_id(0)
    is_ctx = i * tm < n_ctx
    ub = u_ref[...]

    z = cc_ref[...] * ch_ref[...]
    z_before = (ccp_ref[...] * chp_ref[...])[HALO - 1:HALO, :]
    z_after = (ccn_ref[...] * chn_ref[...])[0:1, :]
    row = lax.broadcasted_iota(jnp.int32, z.shape, 0)
    last_pos = jnp.where(is_ctx, seq - 1, dec_seq - 1)
    pos = (i * tm - jnp.where(is_ctx, 0, n_ctx) + row) & last_pos
    z_prev = jnp.where(row == 0, z_before, pltpu.roll(z, 1, 0))
    z_prev = jnp.where(pos == 0, 0.0, z_prev)
    z_next = jnp.where(row == tm - 1, z_after, pltpu.roll(z, tm - 1, 0))
    z_next = jnp.where(pos == last_pos, 0.0, z_next)
    y = z_prev * cw_ref[0:1, :] + z * cw_ref[1:2, :] + z_next * cw_ref[2:3, :] + cbias_ref[...]
    conv_out = cb_ref[...] * y

    ret = jnp.where(is_ctx, retc_ref[...], retl_ref[...])
    na = jnp.where(is_ctx, nac_ref[...], nal_ref[...])
    branches = (ret, conv_out.astype(BF16), na)
    for c in range(D_MODEL // FF_CHUNK):
        merged = None
        for b in range(N_BRANCH):
            sl = slice(b * D_MODEL + c * FF_CHUNK, b * D_MODEL + (c + 1) * FF_CHUNK)
            gate = jax.nn.sigmoid(_dot(ub, wm_ref[:, sl]) + bm_ref[:, sl])
            term = gate * _dot(branches[b], wb_ref[b, :, c * FF_CHUNK:(c + 1) * FF_CHUNK])
            merged = term if merged is None else merged + term
        mb_ref[:, c * FF_CHUNK:(c + 1) * FF_CHUNK] = merged.astype(BF16)
    o_ref[...] = x_ref[...] + mod_ref[5:6, :] * _dot(mb_ref[...], wo_ref[...])


def _rope_tables(length, dim):
    t = jnp.arange(length)
    half = dim // 2
    quarter = half // 2
    inv_freq = ROPE_BASE ** (-jnp.arange(quarter, dtype=F32) * 2.0 / half)

    def tables(pos):
        ang = pos.astype(F32)[:, None] * inv_freq[None, :]
        cos, sin = jnp.cos(ang), jnp.sin(ang)
        return jnp.concatenate([cos, cos], axis=-1), jnp.concatenate([-sin, sin], axis=-1)

    cr, sr = tables(t // GRID_W)
    cc, sc = tables(t % GRID_W)
    return jnp.concatenate([cr, cc], axis=-1), jnp.concatenate([sr, sc], axis=-1)


def _decay_tables(decay_logit):
    c = RET_CHUNK
    log_g = jax.nn.log_sigmoid(decay_logit.astype(F32))
    idx = jnp.arange(c, dtype=F32)
    diff = idx[:, None] - idx[None, :]
    out = []
    for d in range(2):
        lg = log_g[d]
        decay_in = jnp.where(diff >= 0, jnp.exp(lg[:, None, None] * jnp.maximum(diff, 0.0)), 0.0)
        xi = jnp.exp(lg[:, None] * (idx + 1.0))
        zeta = jnp.exp(lg[:, None] * (c - 1.0 - idx))
        g_chunk = jnp.exp(lg * c)
        if d == 1:
            decay_in = jnp.swapaxes(decay_in, 1, 2)
            xi = xi[:, ::-1]
            zeta = zeta[:, ::-1]
        full = (c, c)
        out.append(jnp.stack([decay_in,
                              jnp.broadcast_to(xi[:, :, None], (N_RET_HEADS,) + full),
                              jnp.broadcast_to(zeta[:, :, None], (N_RET_HEADS,) + full),
                              jnp.broadcast_to(g_chunk[:, None, None], (N_RET_HEADS,) + full)], axis=1))
    return jnp.stack(out, axis=0)


def _na_bias_rows(rpb):
    lead = GRID_W - NA_KW
    pad = [(0, 0)] * (rpb.ndim - 1) + [(lead, 2 * GRID_W - lead - rpb.shape[-1])]
    return jnp.pad(rpb.astype(F32) * LOG2E, pad, constant_values=NEG_INF)


def _na_row_offsets():
    rr = np.arange(NA_ROWS)
    kr = np.arange(NA_WIN)
    rel_r = np.stack([rr, rr + NA_KH // 2, rr + NA_WIN - NA_ROWS])
    rel_r0 = np.stack([np.zeros_like(rr), rr, np.full_like(rr, NA_WIN - NA_KH)])
    row_ok = (kr[None, None, :] >= rel_r0[:, :, None]) & (kr[None, None, :] < rel_r0[:, :, None] + NA_KH)
    return np.where(row_ok, kr[None, None, :] - rel_r[:, :, None] + (NA_KH - 1), -1)


def kernel(x_prompt, x_sample, c, state_ret, cache_na_k, cache_na_v, c_ctx, norm_g, w_mod, b_mod, ffn_w1, ffn_w2,
           w_in, ret_decay_logit, conv_w, conv_b, na_rpb, w_branch, w_merge, b_merge, w_out, final_g):
    batch, seq, d = x_prompt.shape
    dec_batch, dec_seq, _ = x_sample.shape
    depth = norm_g.shape[0]
    past_len = cache_na_k.shape[3]
    n_ctx = batch * seq
    n_lat = dec_batch * dec_seq
    n_tok = n_ctx + n_lat
    tm = TOKEN_BLOCK
    tf = FFN_BLOCK
    n_rows = dec_seq // GRID_W
    assert d == D_MODEL and 1 + dec_batch <= COND_ROWS
    assert n_ctx % tm == 0 and dec_seq % tm == 0 and tm % seq == 0 and n_ctx % tf == 0 and dec_seq % tf == 0
    assert seq & (seq - 1) == 0 and dec_seq & (dec_seq - 1) == 0
    assert seq % RET_CHUNK == 0 and dec_seq % RET_CHUNK == 0 and n_ctx % n_lat == 0
    assert n_rows % (NA_ROWS * NA_SUB) == 0 and n_rows >= NA_WIN and D_FF % FF_CHUNK == 0
    n_blocks = n_tok // tm
    ctx_blocks = n_ctx // tm

    x = (x_prompt.reshape(n_ctx, d), x_sample.reshape(n_lat, d))
    cond = jnp.zeros((COND_ROWS, d), F32).at[0].set(c_ctx).at[1:1 + dec_batch].set(c)
    mod = _mod_call(cond, w_mod, b_mod).reshape(depth, COND_ROWS, N_MOD, d)

    w1_b = ffn_w1.astype(BF16)
    w2_b = ffn_w2.astype(BF16)
    w_in_b = w_in.astype(BF16)
    w_merge_b = w_merge.astype(BF16)
    w_branch_b = w_branch.astype(BF16)
    w_out_b = w_out.astype(BF16)
    b_merge3 = b_merge.reshape(depth, 1, N_BRANCH * d)
    conv_b3 = conv_b.reshape(depth, 1, BRANCH_W)
    rope_cos, rope_sin = _rope_tables(dec_seq, RET_DK)
    rope_cos = jnp.concatenate([jnp.ones((tm, RET_DK), F32), rope_cos], axis=0)
    rope_sin = jnp.concatenate([jnp.zeros((tm, RET_DK), F32), rope_sin], axis=0)
    n_chunks_lat = dec_seq // RET_CHUNK
    n_chunks_ctx = seq // RET_CHUNK
    na_steps = n_rows // (NA_ROWS * NA_SUB)
    proj_w = N_SPLIT * BRANCH_W

    def mod_spec(l, rows=tm):
        def index(i):
            row = i * rows
            return (l, jnp.where(row < n_ctx, 0, 1 + (row - n_ctx) // dec_seq), 0, 0)
        return pl.BlockSpec((None, None, N_MOD, d), index)

    def g_spec(l):
        return pl.BlockSpec((None, 3, d), lambda i: (l, 0, 0))

    x_spec = pl.BlockSpec((tm, d), lambda i: (i, 0))
    x_shape = jax.ShapeDtypeStruct((n_tok, d), F32)
    hbm_spec = pl.BlockSpec(memory_space=pl.ANY)

    def ffn(xv, l, s, final=False):
        ctx_steps = n_ctx // tf
        rows_spec = pl.BlockSpec((tf, d), lambda i: (i, 0))
        ctx_spec = pl.BlockSpec((tf, d), lambda i: (jnp.minimum(i, ctx_steps - 1), 0))
        lat_spec = pl.BlockSpec((tf, d), lambda i: (jnp.maximum(i - ctx_steps, 0), 0))
        split_in = isinstance(xv, tuple)
        x_in = list(xv) if split_in else [xv]
        extra_in = [final_g.reshape(1, d)] if final else []
        if final:
            out_specs = [ctx_spec, lat_spec]
            out_shape = [jax.ShapeDtypeStruct((n_ctx, d), F32), jax.ShapeDtypeStruct((n_lat, d), F32)]
        else:
            out_specs, out_shape = rows_spec, x_shape
        return pl.pallas_call(
            functools.partial(_ffn_kernel, gi=2 * s, mi=6 * s, ctx_steps=ctx_steps, split_in=split_in, final=final),
            grid=(n_tok // tf,),
            in_specs=([ctx_spec, lat_spec] if split_in else [rows_spec])
            + [mod_spec(l, tf), g_spec(l),
               _resident((None, None, d, 2 * D_FF), lambda i: (l, s, 0, 0)),
               _resident((None, None, D_FF, d), lambda i: (l, s, 0, 0))]
            + [pl.BlockSpec((1, d), lambda i: (0, 0))] * len(extra_in),
            out_specs=out_specs,
            out_shape=out_shape,
            scratch_shapes=[pltpu.VMEM((tf, d), BF16), pltpu.VMEM((tf, D_FF), BF16)],
            compiler_params=_params(1),
            name="ffn",
        )(*x_in, mod, norm_g, w1_b, w2_b, *extra_in)

    def mixin(xv, l):
        def table_block(i):
            row = i * tm
            return (jnp.where(row < n_ctx, 0, 1 + ((row - n_ctx) % dec_seq) // tm), 0)

        tab = pl.BlockSpec((tm, RET_DK), table_block)
        return pl.pallas_call(
            _mixin_kernel,
            grid=(n_blocks,),
            in_specs=[x_spec, mod_spec(l), g_spec(l), _resident((None, d, proj_w), lambda i: (l, 0, 0)), tab, tab],
            out_specs=[pl.BlockSpec((tm, proj_w), lambda i: (i, 0)), x_spec],
            out_shape=[jax.ShapeDtypeStruct((n_tok, proj_w), F32), jax.ShapeDtypeStruct((n_tok, d), BF16)],
            compiler_params=_params(1),
            name="mixer_in",
        )(xv, mod, norm_g, w_in_b, rope_cos, rope_sin)

    def stacked(shape, l):
        zeros = (0,) * len(shape)
        if l == 0:
            return pl.BlockSpec((None, depth) + shape, lambda b: (b, 0) + zeros)
        return pl.BlockSpec((None, None) + shape, lambda b: (b, l) + zeros)

    def ret_ctx(proj, dec, states, l):
        col = lambda k: pl.BlockSpec((seq, BRANCH_W), lambda b: (b, k))
        state_shape = (2, N_RET_HEADS, RET_DK, RET_DV)
        carried = () if l == 0 else (states,)
        return pl.pallas_call(
            functools.partial(_ret_ctx_kernel, n_chunks=n_chunks_ctx, first_layer=l == 0),
            grid=(batch,),
            in_specs=[col(0), col(1), col(2), col(3),
                      _resident((None, 2, N_RET_HEADS, 4, RET_CHUNK, RET_CHUNK), lambda b: (l, 0, 0, 0, 0, 0))]
            + [hbm_spec] * len(carried),
            out_specs=[pl.BlockSpec((seq, BRANCH_W), lambda b: (b, 0)), stacked(state_shape, l)],
            out_shape=[jax.ShapeDtypeStruct((n_ctx, BRANCH_W), BF16),
                       jax.ShapeDtypeStruct((batch, depth) + state_shape, F32)],
            input_output_aliases={5: 1} if carried else {},
            compiler_params=_params(1),
            name="retention_ctx",
        )(proj, proj, proj, proj, dec, *carried)

    def ret_lat(proj, dec, l):
        proj4 = proj.reshape(n_tok // n_lat, dec_batch, dec_seq, proj_w)
        lat = n_ctx // n_lat

        def chunk_of(p, ci):
            return jnp.where(p == 0, ci, n_chunks_lat - 1 - ci)

        col = lambda k: pl.BlockSpec((None, dec_batch, RET_CHUNK, BRANCH_W),
                                     lambda p, ci: (lat, 0, chunk_of(p, ci), k))
        return pl.pallas_call(
            functools.partial(_ret_lat_kernel, n_chunks=n_chunks_lat, n_seqs=dec_batch),
            grid=(2, n_chunks_lat),
            in_specs=[col(0), col(1), col(2), col(3),
                      pl.BlockSpec((dec_batch, None, None, N_RET_HEADS, RET_DK, RET_DV),
                                   lambda p, ci: (0, l, p, 0, 0, 0)),
                      pl.BlockSpec((None, None, N_RET_HEADS, 4, RET_CHUNK, RET_CHUNK),
                                   lambda p, ci: (l, p, 0, 0, 0, 0))],
            out_specs=pl.BlockSpec((dec_batch, RET_CHUNK, BRANCH_W),
                                   lambda p, ci: (0, n_chunks_lat - 1 - jnp.where(p == 0, 0, ci), 0)),
            out_shape=jax.ShapeDtypeStruct((dec_batch, dec_seq, BRANCH_W), BF16),
            scratch_shapes=[pltpu.VMEM((dec_batch, N_RET_HEADS, RET_DK, RET_DV), F32),
                            pltpu.VMEM((dec_batch, dec_seq, BRANCH_W), F32)],
            compiler_params=_params(2),
            name="retention_lat",
        )(proj4, proj4, proj4, proj4, state_ret, dec).reshape(n_lat, BRANCH_W)

    def na_ctx(proj, ck, cv, l):
        col = lambda k: pl.BlockSpec((seq, BRANCH_W), lambda b: (b, k))
        slab = (N_NA_HEADS, seq, NA_HEAD_DIM)
        cache_shape = jax.ShapeDtypeStruct((batch, depth) + slab, F32)
        carried = () if l == 0 else (ck, cv)
        return pl.pallas_call(
            functools.partial(_ctx_attn_kernel, first_layer=l == 0),
            grid=(batch,),
            in_specs=[col(7), col(8), col(9)] + [hbm_spec] * len(carried),
            out_specs=[pl.BlockSpec((seq, BRANCH_W), lambda b: (b, 0)), stacked(slab, l), stacked(slab, l)],
            out_shape=[jax.ShapeDtypeStruct((n_ctx, BRANCH_W), BF16), cache_shape, cache_shape],
            input_output_aliases={3: 1, 4: 2} if carried else {},
            compiler_params=_params(1),
            name="attention_ctx",
        )(proj, proj, proj, *carried)

    def na_lat(proj, rpb_rows, l):
        q_rows = NA_SUB * NA_ROWS * GRID_W
        pairs = N_NA_HEADS // 2
        q_base = n_ctx // q_rows
        kv_base = n_ctx // dec_seq
        cache_spec = pl.BlockSpec((None, None, 2, past_len, NA_HEAD_DIM), lambda b, hp, i: (b, l, hp, 0, 0))
        return pl.pallas_call(
            functools.partial(_na_kernel, n_rows=n_rows),
            grid=(dec_batch, pairs, na_steps),
            in_specs=[pl.BlockSpec((q_rows, LANES), lambda b, hp, i: (q_base + b * na_steps + i, 7 * pairs + hp)),
                      pl.BlockSpec((dec_seq, LANES), lambda b, hp, i: (kv_base + b, 8 * pairs + hp)),
                      pl.BlockSpec((dec_seq, LANES), lambda b, hp, i: (kv_base + b, 9 * pairs + hp)),
                      cache_spec, cache_spec,
                      pl.BlockSpec((None, 2, 2 * NA_KH - 1, 2 * GRID_W), lambda b, hp, i: (l, hp, 0, 0))],
            out_specs=pl.BlockSpec((q_rows, LANES), lambda b, hp, i: (b * na_steps + i, hp)),
            out_shape=jax.ShapeDtypeStruct((n_lat, BRANCH_W), BF16),
            scratch_shapes=[pltpu.VMEM((2, 3, NA_ROWS * GRID_W, NA_WIN * GRID_W), F32)],
            compiler_params=_params(3),
            name="attention_lat",
        )(proj, proj, proj, cache_na_k, cache_na_v, rpb_rows)

    def mixout(xv, ub, proj, ret_c, ret_l, na_c, na_l, l):
        col = lambda k: pl.BlockSpec((tm, BRANCH_W), lambda i: (i, k))
        before = lambda k: pl.BlockSpec((HALO, BRANCH_W), lambda i: (jnp.maximum(i * (tm // HALO) - 1, 0), k))
        after = lambda k: pl.BlockSpec((HALO, BRANCH_W),
                                       lambda i: (jnp.minimum((i + 1) * (tm // HALO), n_tok // HALO - 1), k))
        ctx_rows = pl.BlockSpec((tm, BRANCH_W), lambda i: (jnp.minimum(i, ctx_blocks - 1), 0))
        lat_rows = pl.BlockSpec((tm, BRANCH_W), lambda i: (jnp.maximum(i - ctx_blocks, 0), 0))
        return pl.pallas_call(
            functools.partial(_mixout_kernel, tm=tm, n_ctx=n_ctx, seq=seq, dec_seq=dec_seq),
            grid=(n_blocks,),
            in_specs=[x_spec, x_spec, mod_spec(l), col(4), col(5), col(6), before(5), before(6), after(5),
                      after(6), ctx_rows, lat_rows, ctx_rows, lat_rows,
                      pl.BlockSpec((None, CONV_K, BRANCH_W), lambda i: (l, 0, 0)),
                      pl.BlockSpec((None, 1, BRANCH_W), lambda i: (l, 0, 0)),
                      _resident((None, d, N_BRANCH * d), lambda i: (l, 0, 0)),
                      pl.BlockSpec((None, 1, N_BRANCH * d), lambda i: (l, 0, 0)),
                      _resident((None, N_BRANCH, BRANCH_W, d), lambda i: (l, 0, 0, 0)),
                      _resident((None, d, d), lambda i: (l, 0, 0))],
            out_specs=x_spec,
            out_shape=x_shape,
            scratch_shapes=[pltpu.VMEM((tm, d), BF16)],
            compiler_params=_params(1),
            name="mixer_out",
        )(xv, ub, mod, proj, proj, proj, proj, proj, proj, proj, ret_c, ret_l, na_c, na_l, conv_w, conv_b3,
          w_merge_b, b_merge3, w_branch_b, w_out_b)

    dec_all = jax.vmap(_decay_tables)(ret_decay_logit)
    rpb_rows = _na_bias_rows(na_rpb)
    states = cache_k = cache_v = None
    for l in range(depth):
        x = ffn(x, l, 0)
        proj, ub = mixin(x, l)
        ret_c, states = ret_ctx(proj, dec_all, states, l)
        ret_l = ret_lat(proj, dec_all, l)
        na_c, cache_k, cache_v = na_ctx(proj, cache_k, cache_v, l)
        na_l = na_lat(proj, rpb_rows, l)
        x = mixout(x, ub, proj, ret_c, ret_l, na_c, na_l, l)
        x = ffn(x, l, 1, final=l == depth - 1)

    y_prompt, y_sample = x
    return (y_prompt.reshape(batch, seq, d), y_sample.reshape(dec_batch, dec_seq, d), states, cache_k, cache_v)
```

```python
import functools

import jax
import jax.numpy as jnp
import numpy as np
from jax import lax
from jax.experimental import pallas as pl
from jax.experimental.pallas import tpu as pltpu

D_MODEL = 1024
GRID_W = 64
N_RET_HEADS = 4
RET_DK = 128
RET_DV = 128
RET_CHUNK = 128
CONV_K = 3
N_NA_HEADS = 8
NA_HEAD_DIM = 64
NA_KH = 8
NA_KW = 16
BRANCH_W = 512
N_BRANCH = 3
N_SPLIT = 10
D_FF = 2816
N_MOD = 9
ROPE_BASE = 10000.0
EPS = 1e-6
NEG_INF = -1e30
LOG2E = 1.4426950408889634

BF16 = jnp.bfloat16
F32 = jnp.float32

VMEM_LIMIT_BYTES = 56 * 1024 * 1024
LANES = 128
COND_ROWS = 8
TOKEN_BLOCK = 512
FFN_BLOCK = 1024
FF_CHUNK = 256
RET_STEP = 2
NA_ROWS = 4
NA_WIN = NA_ROWS + NA_KH - 1
NA_SUB = 4
HALO = 8


def _params(n_axes):
    return pltpu.CompilerParams(dimension_semantics=("arbitrary",) * n_axes,
                                vmem_limit_bytes=VMEM_LIMIT_BYTES)


def _resident(block_shape, index_map):
    return pl.BlockSpec(block_shape, index_map, pipeline_mode=pl.Buffered(1))


def _dot(a, b):
    return jnp.dot(a, b, preferred_element_type=F32)


def _dot_nt(a, b):
    return lax.dot_general(a, b, (((1,), (1,)), ((), ())), preferred_element_type=F32)


def _norm_mod(x, g_row, scale_row, shift_row):
    ms = jnp.mean(x * x, axis=-1, keepdims=True)
    y = x * lax.rsqrt(ms + EPS)
    return (y * g_row) * (1.0 + scale_row) + shift_row


def _mod_kernel(cond_ref, w_ref, b_ref, o_ref):
    a = jax.nn.silu(cond_ref[...]).astype(BF16)
    o_ref[...] = _dot(a, w_ref[...].astype(BF16)) + b_ref[...]


def _mod_call(cond, w_mod, b_mod):
    depth, d, n = w_mod.shape
    tn = 1024
    return pl.pallas_call(
        _mod_kernel,
        grid=(depth, n // tn),
        in_specs=[pl.BlockSpec((COND_ROWS, d), lambda l, j: (0, 0)),
                  pl.BlockSpec((None, d, tn), lambda l, j: (l, 0, j)),
                  pl.BlockSpec((None, 1, tn), lambda l, j: (l, 0, j))],
        out_specs=pl.BlockSpec((None, COND_ROWS, tn), lambda l, j: (l, 0, j)),
        out_shape=jax.ShapeDtypeStruct((depth, COND_ROWS, n), F32),
        compiler_params=_params(2),
        name="mod_vectors",
    )(cond, w_mod, b_mod.reshape(depth, 1, n))


def _ffn_kernel(*refs, gi, mi, ctx_steps, split_in, final):
    refs = list(refs)
    x_refs = [refs.pop(0) for _ in range(2 if split_in else 1)]
    mod_ref, g_ref, w1_ref, w2_ref = (refs.pop(0) for _ in range(4))
    fg_ref = refs.pop(0) if final else None
    o_refs = [refs.pop(0) for _ in range(2 if final else 1)]
    hb_ref, t_ref = refs
    is_ctx = pl.program_id(0) < ctx_steps

    def rows():
        return jnp.where(is_ctx, x_refs[0][...], x_refs[1][...]) if split_in else x_refs[0][...]

    h = _norm_mod(rows(), g_ref[gi:gi + 1, :], mod_ref[mi + 1:mi + 2, :], mod_ref[mi:mi + 1, :])
    hb_ref[...] = h.astype(BF16)
    for j in range(D_FF // FF_CHUNK):
        hb = hb_ref[...]
        a = _dot(hb, w1_ref[:, j * FF_CHUNK:(j + 1) * FF_CHUNK])
        b = _dot(hb, w1_ref[:, D_FF + j * FF_CHUNK:D_FF + (j + 1) * FF_CHUNK])
        t_ref[:, j * FF_CHUNK:(j + 1) * FF_CHUNK] = (jax.nn.silu(a) * b).astype(BF16)
    x_new = rows() + (0.5 * mod_ref[mi + 2:mi + 3, :]) * _dot(t_ref[...], w2_ref[...])
    if not final:
        o_refs[0][...] = x_new
        return
    ms = jnp.mean(x_new * x_new, axis=-1, keepdims=True)
    y = (x_new * lax.rsqrt(ms + EPS)) * fg_ref[...]

    @pl.when(is_ctx)
    def _():
        o_refs[0][...] = y

    @pl.when(jnp.logical_not(is_ctx))
    def _():
        o_refs[1][...] = y


def _mixin_kernel(x_ref, mod_ref, g_ref, w_ref, cos_ref, sin_ref, o_ref, u_ref):
    u = _norm_mod(x_ref[...], g_ref[1:2, :], mod_ref[4:5, :], mod_ref[3:4, :]).astype(BF16)
    u_ref[...] = u
    cos = cos_ref[...]
    sin = sin_ref[...]
    lane = lax.broadcasted_iota(jnp.int32, cos.shape, 1)
    first = (lane % (RET_DK // 2)) < (RET_DK // 4)
    for k in range(N_SPLIT):
        sl = slice(k * BRANCH_W, (k + 1) * BRANCH_W)
        r = _dot(u, w_ref[:, sl])
        if k < 2:
            heads = []
            for h in range(N_RET_HEADS):
                xh = r[:, h * RET_DK:(h + 1) * RET_DK]
                if k == 0:
                    xh = xh * (RET_DK ** -0.5)
                swapped = jnp.where(first, pltpu.roll(xh, RET_DK - RET_DK // 4, 1), pltpu.roll(xh, RET_DK // 4, 1))
                heads.append(xh * cos + swapped * sin)
            r = jnp.concatenate(heads, axis=-1)
        o_ref[:, sl] = r


def _layer_slab(refs, first_layer):
    if not first_layer:
        return refs
    for r in refs:
        if r.shape[0] > 1:
            r[1:] = jnp.zeros((r.shape[0] - 1,) + r.shape[1:], r.dtype)
    return [r.at[0] for r in refs]


def _decayed_keys_t(k_tiles, zeta_tiles):
    kz = jnp.concatenate([k * z for k, z in zip(k_tiles, zeta_tiles)], axis=-1)
    kzt = kz.T.astype(BF16)
    return [kzt[i * RET_DK:(i + 1) * RET_DK] for i in range(len(k_tiles))]


def _ret_level(items):
    stage = []
    for q, k, kzt, v, s, dm, xi, gc in items:
        qb = q.astype(BF16)
        vb = v.astype(BF16)
        stage.append((vb, s, dm, xi, gc, _dot_nt(qb, k.astype(BF16)), _dot(qb, s.astype(BF16)), _dot(kzt, vb)))
    out = []
    for vb, s, dm, xi, gc, qk, qs, kv in stage:
        s_new = s * gc + kv
        out.append((_dot((qk * dm).astype(BF16), vb) + qs * xi, s_new))
    return out


def _head_norm_gate(os, gs):
    mus = [jnp.mean(o, axis=-1, keepdims=True) for o in os]
    ds = [o - mu for o, mu in zip(os, mus)]
    vs = [jnp.mean(jnp.square(d), axis=-1, keepdims=True) for d in ds]
    return [(d * lax.rsqrt(v + EPS)) * jax.nn.silu(g) for d, v, g in zip(ds, vs, gs)]


def _ret_ctx_kernel(q_ref, k_ref, v_ref, g_ref, dec_ref, *rest, n_chunks, first_layer):
    st_ref = _layer_slab(rest[-1:], first_layer)[0]
    o_ref = rest[-2]
    c = RET_CHUNK
    heads = [slice(h * RET_DK, (h + 1) * RET_DK) for h in range(N_RET_HEADS)]
    rows = [slice(ci * c, (ci + 1) * c) for ci in range(n_chunks)]
    chains = [(d, h) for d in range(2) for h in range(N_RET_HEADS)]
    state = {dh: jnp.zeros((RET_DK, RET_DV), F32) for dh in chains}
    o_tot = {}
    for t in range(n_chunks):
        cis = {(d, h): (t if d == 0 else n_chunks - 1 - t) for d, h in chains}
        kzt = _decayed_keys_t([k_ref[rows[cis[d, h]], heads[h]] for d, h in chains],
                              [dec_ref[d, h, 2] for d, h in chains])
        items = [(q_ref[rows[cis[d, h]], heads[h]], k_ref[rows[cis[d, h]], heads[h]], kzt_dh,
                  v_ref[rows[cis[d, h]], heads[h]], state[d, h], dec_ref[d, h, 0], dec_ref[d, h, 1], dec_ref[d, h, 3])
                 for (d, h), kzt_dh in zip(chains, kzt)]
        for (d, h), (o, s_new) in zip(chains, _ret_level(items)):
            state[d, h] = s_new
            key = (cis[d, h], h)
            o_tot[key] = o if key not in o_tot else o_tot[key] + o
    for d, h in chains:
        st_ref[d, h] = state[d, h]
    tiles = [(ci, h) for ci in range(n_chunks) for h in range(N_RET_HEADS)]
    normed = _head_norm_gate([o_tot[t] for t in tiles], [g_ref[rows[ci], heads[h]] for ci, h in tiles])
    for (ci, h), y in zip(tiles, normed):
        o_ref[rows[ci], heads[h]] = y.astype(o_ref.dtype)


def _ret_lat_kernel(q_ref, k_ref, v_ref, g_ref, s0_ref, dec_ref, o_ref, s_scr, of_scr, *, n_steps, n_seqs):
    p = pl.program_id(0)
    si = pl.program_id(1)

    @pl.when(si == 0)
    def _():
        s_scr[...] = s0_ref[...]

    block = jnp.where(p == 0, si, n_steps - 1 - si)
    heads = [slice(h * RET_DK, (h + 1) * RET_DK) for h in range(N_RET_HEADS)]
    tiles = [(b, h) for b in range(n_seqs) for h in range(N_RET_HEADS)]
    state = {bh: s_scr[bh] for bh in tiles}
    done = []
    for t in range(RET_STEP):
        first_row = pl.multiple_of(jnp.where(p == 0, t, RET_STEP - 1 - t) * RET_CHUNK, RET_CHUNK)
        rows = pl.ds(first_row, RET_CHUNK)
        kzt = _decayed_keys_t([k_ref[b, rows, heads[h]] for b, h in tiles], [dec_ref[h, 2] for _, h in tiles])
        items = [(q_ref[b, rows, heads[h]], k_ref[b, rows, heads[h]], kzt_bh, v_ref[b, rows, heads[h]], state[b, h],
                  dec_ref[h, 0], dec_ref[h, 1], dec_ref[h, 3]) for (b, h), kzt_bh in zip(tiles, kzt)]
        os = []
        for bh, (o, s_new) in zip(tiles, _ret_level(items)):
            state[bh] = s_new
            os.append(o)
        done.append((first_row, rows, os))
    for bh in tiles:
        s_scr[bh] = state[bh]

    def parked(first_row):
        return pl.ds(pl.multiple_of(block * (RET_STEP * RET_CHUNK) + first_row, RET_CHUNK), RET_CHUNK)

    @pl.when(p == 0)
    def _():
        for first_row, _, os in done:
            for (b, h), o in zip(tiles, os):
                of_scr[b, parked(first_row), heads[h]] = o

    @pl.when(p == 1)
    def _():
        for first_row, rows, os in done:
            tots = [of_scr[b, parked(first_row), heads[h]] + o for (b, h), o in zip(tiles, os)]
            normed = _head_norm_gate(tots, [g_ref[b, rows, heads[h]] for b, h in tiles])
            for (b, h), y in zip(tiles, normed):
                o_ref[b, rows, heads[h]] = y.astype(o_ref.dtype)


def _softmax_parts(parts):
    m = functools.reduce(jnp.maximum, [jnp.max(s, axis=-1, keepdims=True) for s in parts])
    es = [jnp.exp2(s - m) for s in parts]
    den = functools.reduce(lambda a, b: a + b, [jnp.sum(e, axis=-1, keepdims=True) for e in es])
    return es, 1.0 / den


def _split_pair(q2):
    low = lax.broadcasted_iota(jnp.int32, q2.shape, 1) < NA_HEAD_DIM
    return low, (jnp.where(low, q2, 0.0).astype(BF16), jnp.where(low, 0.0, q2).astype(BF16))


def _ctx_attn_kernel(q_ref, k_ref, v_ref, *rest, first_layer):
    o_ref = rest[-3]
    ck_ref, cv_ref = _layer_slab(rest[-2:], first_layer)
    pairs = [slice(hp * LANES, (hp + 1) * LANES) for hp in range(N_NA_HEADS // 2)]
    scores = []
    for hp, sl in enumerate(pairs):
        k2 = k_ref[:, sl]
        v2 = v_ref[:, sl]
        for hh in range(2):
            ck_ref[2 * hp + hh] = k2[:, hh * NA_HEAD_DIM:(hh + 1) * NA_HEAD_DIM]
            cv_ref[2 * hp + hh] = v2[:, hh * NA_HEAD_DIM:(hh + 1) * NA_HEAD_DIM]
        kb = k2.astype(BF16)
        low, q_heads = _split_pair(q_ref[:, sl] * (NA_HEAD_DIM ** -0.5 * LOG2E))
        scores += [_dot_nt(qh, kb) for qh in q_heads]
    weights = [_softmax_parts([s]) for s in scores]
    for hp, sl in enumerate(pairs):
        vb = v_ref[:, sl].astype(BF16)
        outs = []
        for hh in range(2):
            (e,), inv = weights[2 * hp + hh]
            outs.append(_dot(e.astype(BF16), vb) * inv)
        o_ref[:, sl] = jnp.where(low, outs[0], outs[1]).astype(o_ref.dtype)


def _na_kernel(q_ref, k_ref, v_ref, kc_ref, vc_ref, rpb_ref, o_ref, bias_ref, *, n_rows):
    i = pl.program_id(2)
    q_rows = NA_ROWS * GRID_W
    n_groups = n_rows // NA_ROWS

    @pl.when(i == 0)
    def _():
        w = GRID_W
        masked = jnp.full((w, w), NEG_INF, F32)
        qc = lax.broadcasted_iota(jnp.int32, (w, w), 0)
        kc = lax.broadcasted_iota(jnp.int32, (w, w), 1)
        c0 = jnp.clip(qc - NA_KW // 2, 0, w - NA_KW)
        col_ok = (kc >= c0) & (kc < c0 + NA_KW)
        offsets = _na_row_offsets()
        for hh in range(2):
            tiles = {}
            for dr in sorted({int(o) for o in offsets.ravel() if o >= 0}):
                rows = jnp.broadcast_to(rpb_ref[hh, dr:dr + 1, :], (w, 2 * w))
                skewed = pltpu.roll(rows, w + 1, 1, stride=1, stride_axis=0)
                tiles[dr] = jnp.where(col_ok, skewed[:, :w], NEG_INF)
            for p in range(3):
                for r in range(NA_ROWS):
                    row = [masked if offsets[p, r, k] < 0 else tiles[int(offsets[p, r, k])] for k in range(NA_WIN)]
                    bias_ref[hh, p, r * w:(r + 1) * w, :] = jnp.concatenate(row, axis=-1)

    kcb = [kc_ref[hh].astype(BF16) for hh in range(2)]
    vcb = [vc_ref[hh].astype(BF16) for hh in range(2)]
    scores = []
    for j in range(NA_SUB):
        grp = i * NA_SUB + j
        pattern = jnp.where(grp == 0, 0, jnp.where(grp == n_groups - 1, 2, 1))
        win0 = jnp.clip(NA_ROWS * grp - NA_KH // 2, 0, n_rows - NA_WIN)
        start = pl.multiple_of(win0 * GRID_W, GRID_W)
        kw = k_ref[pl.ds(start, NA_WIN * GRID_W), :].astype(BF16)
        q2 = q_ref[j * q_rows:(j + 1) * q_rows, :] * (NA_HEAD_DIM ** -0.5 * LOG2E)
        low, q_heads = _split_pair(q2)
        for hh in range(2):
            s_loc = _dot_nt(q_heads[hh], kw) + bias_ref[hh, pattern]
            s_ctx = _dot_nt(q2[:, hh * NA_HEAD_DIM:(hh + 1) * NA_HEAD_DIM].astype(BF16), kcb[hh])
            scores.append((s_loc, s_ctx))
    weights = [_softmax_parts(list(sc)) for sc in scores]
    for j in range(NA_SUB):
        grp = i * NA_SUB + j
        win0 = jnp.clip(NA_ROWS * grp - NA_KH // 2, 0, n_rows - NA_WIN)
        start = pl.multiple_of(win0 * GRID_W, GRID_W)
        vw = v_ref[pl.ds(start, NA_WIN * GRID_W), :].astype(BF16)
        loc, ctx = [], []
        for hh in range(2):
            (e_loc, e_ctx), inv = weights[2 * j + hh]
            loc.append(_dot(e_loc.astype(BF16), vw) * inv)
            ctx.append(_dot(e_ctx.astype(BF16), vcb[hh]) * inv)
        low = lax.broadcasted_iota(jnp.int32, loc[0].shape, 1) < NA_HEAD_DIM
        o_ref[j * q_rows:(j + 1) * q_rows, :] = (jnp.where(low, loc[0], loc[1])
                                                 + jnp.concatenate(ctx, axis=-1)).astype(o_ref.dtype)


def _mixout_kernel(x_ref, u_ref, mod_ref, cb_ref, cc_ref, ch_ref, ccp_ref, chp_ref, ccn_ref, chn_ref,
                   retc_ref, retl_ref, nac_ref, nal_ref, cw_ref, cbias_ref, wm_ref, bm_ref, wb_ref, wo_ref,
                   o_ref, mb_ref, *, tm, n_ctx, seq, dec_seq):
    i = pl.program_id(0)
    is_ctx = i * tm < n_ctx
    ub = u_ref[...]

    z = cc_ref[...] * ch_ref[...]
    z_before = (ccp_ref[...] * chp_ref[...])[HALO - 1:HALO, :]
    z_after = (ccn_ref[...] * chn_ref[...])[0:1, :]
    row = lax.broadcasted_iota(jnp.int32, z.shape, 0)
    last_pos = jnp.where(is_ctx, seq - 1, dec_seq - 1)
    pos = (i * tm - jnp.where(is_ctx, 0, n_ctx) + row) & last_pos
    z_prev = jnp.where(row == 0, z_before, pltpu.roll(z, 1, 0))
    z_prev = jnp.where(pos == 0, 0.0, z_prev)
    z_next = jnp.where(row == tm - 1, z_after, pltpu.roll(z, tm - 1, 0))
    z_next = jnp.where(pos == last_pos, 0.0, z_next)
    y = z_prev * cw_ref[0:1, :] + z * cw_ref[1:2, :] + z_next * cw_ref[2:3, :] + cbias_ref[...]
    conv_out = cb_ref[...] * y

    ret = jnp.where(is_ctx, retc_ref[...], retl_ref[...])
    na = jnp.where(is_ctx, nac_ref[...], nal_ref[...])
    branches = (ret, conv_out.astype(BF16), na)
    for c in range(D_MODEL // FF_CHUNK):
        merged = None
        for b in range(N_BRANCH):
            sl = slice(b * D_MODEL + c * FF_CHUNK, b * D_MODEL + (c + 1) * FF_CHUNK)
            gate = jax.nn.sigmoid(_dot(ub, wm_ref[:, sl]) + bm_ref[:, sl])
            term = gate * _dot(branches[b], wb_ref[b, :, c * FF_CHUNK:(c + 1) * FF_CHUNK])
            merged = term if merged is None else merged + term
        mb_ref[:, c * FF_CHUNK:(c + 1) * FF_CHUNK] = merged.astype(BF16)
    o_ref[...] = x_ref[...] + mod_ref[5:6, :] * _dot(mb_ref[...], wo_ref[...])


def _rope_tables(length, dim):
    t = jnp.arange(length)
    half = dim // 2
    quarter = half // 2
    inv_freq = ROPE_BASE ** (-jnp.arange(quarter, dtype=F32) * 2.0 / half)

    def tables(pos):
        ang = pos.astype(F32)[:, None] * inv_freq[None, :]
        cos, sin = jnp.cos(ang), jnp.sin(ang)
        return jnp.concatenate([cos, cos], axis=-1), jnp.concatenate([-sin, sin], axis=-1)

    cr, sr = tables(t // GRID_W)
    cc, sc = tables(t % GRID_W)
    return jnp.concatenate([cr, cc], axis=-1), jnp.concatenate([sr, sc], axis=-1)


def _decay_tables(decay_logit):
    c = RET_CHUNK
    log_g = jax.nn.log_sigmoid(decay_logit.astype(F32))
    idx = jnp.arange(c, dtype=F32)
    diff = idx[:, None] - idx[None, :]
    out = []
    for d in range(2):
        lg = log_g[d]
        decay_in = jnp.where(diff >= 0, jnp.exp(lg[:, None, None] * jnp.maximum(diff, 0.0)), 0.0)
        xi = jnp.exp(lg[:, None] * (idx + 1.0))
        zeta = jnp.exp(lg[:, None] * (c - 1.0 - idx))
        g_chunk = jnp.exp(lg * c)
        if d == 1:
            decay_in = jnp.swapaxes(decay_in, 1, 2)
            xi = xi[:, ::-1]
            zeta = zeta[:, ::-1]
        full = (c, c)
        out.append(jnp.stack([decay_in,
                              jnp.broadcast_to(xi[:, :, None], (N_RET_HEADS,) + full),
                              jnp.broadcast_to(zeta[:, :, None], (N_RET_HEADS,) + full),
                              jnp.broadcast_to(g_chunk[:, None, None], (N_RET_HEADS,) + full)], axis=1))
    return jnp.stack(out, axis=0)


def _na_bias_rows(rpb):
    lead = GRID_W - NA_KW
    pad = [(0, 0)] * (rpb.ndim - 1) + [(lead, 2 * GRID_W - lead - rpb.shape[-1])]
    return jnp.pad(rpb.astype(F32) * LOG2E, pad, constant_values=NEG_INF)


def _na_row_offsets():
    rr = np.arange(NA_ROWS)
    kr = np.arange(NA_WIN)
    rel_r = np.stack([rr, rr + NA_KH // 2, rr + NA_WIN - NA_ROWS])
    rel_r0 = np.stack([np.zeros_like(rr), rr, np.full_like(rr, NA_WIN - NA_KH)])
    row_ok = (kr[None, None, :] >= rel_r0[:, :, None]) & (kr[None, None, :] < rel_r0[:, :, None] + NA_KH)
    return np.where(row_ok, kr[None, None, :] - rel_r[:, :, None] + (NA_KH - 1), -1)


def kernel(x_prompt, x_sample, c, state_ret, cache_na_k, cache_na_v, c_ctx, norm_g, w_mod, b_mod, ffn_w1, ffn_w2,
           w_in, ret_decay_logit, conv_w, conv_b, na_rpb, w_branch, w_merge, b_merge, w_out, final_g):
    batch, seq, d = x_prompt.shape
    dec_batch, dec_seq, _ = x_sample.shape
    depth = norm_g.shape[0]
    past_len = cache_na_k.shape[3]
    n_ctx = batch * seq
    n_lat = dec_batch * dec_seq
    n_tok = n_ctx + n_lat
    tm = TOKEN_BLOCK
    tf = FFN_BLOCK
    n_rows = dec_seq // GRID_W
    assert d == D_MODEL and 1 + dec_batch <= COND_ROWS
    assert n_ctx % tm == 0 and dec_seq % tm == 0 and tm % seq == 0 and n_ctx % tf == 0 and dec_seq % tf == 0
    assert seq & (seq - 1) == 0 and dec_seq & (dec_seq - 1) == 0
    assert seq % RET_CHUNK == 0 and dec_seq % (RET_STEP * RET_CHUNK) == 0 and n_ctx % n_lat == 0
    assert n_rows % (NA_ROWS * NA_SUB) == 0 and n_rows >= NA_WIN and D_FF % FF_CHUNK == 0
    n_blocks = n_tok // tm
    ctx_blocks = n_ctx // tm

    x = (x_prompt.reshape(n_ctx, d), x_sample.reshape(n_lat, d))
    cond = jnp.zeros((COND_ROWS, d), F32).at[0].set(c_ctx).at[1:1 + dec_batch].set(c)
    mod = _mod_call(cond, w_mod, b_mod).reshape(depth, COND_ROWS, N_MOD, d)

    w1_b = ffn_w1.astype(BF16)
    w2_b = ffn_w2.astype(BF16)
    w_in_b = w_in.astype(BF16)
    w_merge_b = w_merge.astype(BF16)
    w_branch_b = w_branch.astype(BF16)
    w_out_b = w_out.astype(BF16)
    b_merge3 = b_merge.reshape(depth, 1, N_BRANCH * d)
    conv_b3 = conv_b.reshape(depth, 1, BRANCH_W)
    rope_cos, rope_sin = _rope_tables(dec_seq, RET_DK)
    rope_cos = jnp.concatenate([jnp.ones((tm, RET_DK), F32), rope_cos], axis=0)
    rope_sin = jnp.concatenate([jnp.zeros((tm, RET_DK), F32), rope_sin], axis=0)
    n_chunks_lat = dec_seq // RET_CHUNK
    n_chunks_ctx = seq // RET_CHUNK
    na_steps = n_rows // (NA_ROWS * NA_SUB)
    proj_w = N_SPLIT * BRANCH_W

    def mod_spec(l, rows=tm):
        def index(i):
            row = i * rows
            return (l, jnp.where(row < n_ctx, 0, 1 + (row - n_ctx) // dec_seq), 0, 0)
        return pl.BlockSpec((None, None, N_MOD, d), index)

    def g_spec(l):
        return pl.BlockSpec((None, 3, d), lambda i: (l, 0, 0))

    x_spec = pl.BlockSpec((tm, d), lambda i: (i, 0))
    x_shape = jax.ShapeDtypeStruct((n_tok, d), F32)
    hbm_spec = pl.BlockSpec(memory_space=pl.ANY)

    def ffn(xv, l, s, final=False):
        ctx_steps = n_ctx // tf
        rows_spec = pl.BlockSpec((tf, d), lambda i: (i, 0))
        ctx_spec = pl.BlockSpec((tf, d), lambda i: (jnp.minimum(i, ctx_steps - 1), 0))
        lat_spec = pl.BlockSpec((tf, d), lambda i: (jnp.maximum(i - ctx_steps, 0), 0))
        split_in = isinstance(xv, tuple)
        x_in = list(xv) if split_in else [xv]
        extra_in = [final_g.reshape(1, d)] if final else []
        if final:
            out_specs = [ctx_spec, lat_spec]
            out_shape = [jax.ShapeDtypeStruct((n_ctx, d), F32), jax.ShapeDtypeStruct((n_lat, d), F32)]
        else:
            out_specs, out_shape = rows_spec, x_shape
        return pl.pallas_call(
            functools.partial(_ffn_kernel, gi=2 * s, mi=6 * s, ctx_steps=ctx_steps, split_in=split_in, final=final),
            grid=(n_tok // tf,),
            in_specs=([ctx_spec, lat_spec] if split_in else [rows_spec])
            + [mod_spec(l, tf), g_spec(l),
               _resident((None, None, d, 2 * D_FF), lambda i: (l, s, 0, 0)),
               _resident((None, None, D_FF, d), lambda i: (l, s, 0, 0))]
            + [pl.BlockSpec((1, d), lambda i: (0, 0))] * len(extra_in),
            out_specs=out_specs,
            out_shape=out_shape,
            scratch_shapes=[pltpu.VMEM((tf, d), BF16), pltpu.VMEM((tf, D_FF), BF16)],
            compiler_params=_params(1),
            name="ffn",
        )(*x_in, mod, norm_g, w1_b, w2_b, *extra_in)

    def mixin(xv, l):
        def table_block(i):
            row = i * tm
            return (jnp.where(row < n_ctx, 0, 1 + ((row - n_ctx) % dec_seq) // tm), 0)

        tab = pl.BlockSpec((tm, RET_DK), table_block)
        return pl.pallas_call(
            _mixin_kernel,
            grid=(n_blocks,),
            in_specs=[x_spec, mod_spec(l), g_spec(l), _resident((None, d, proj_w), lambda i: (l, 0, 0)), tab, tab],
            out_specs=[pl.BlockSpec((tm, proj_w), lambda i: (i, 0)), x_spec],
            out_shape=[jax.ShapeDtypeStruct((n_tok, proj_w), F32), jax.ShapeDtypeStruct((n_tok, d), BF16)],
            compiler_params=_params(1),
            name="mixer_in",
        )(xv, mod, norm_g, w_in_b, rope_cos, rope_sin)

    def stacked(shape, l):
        zeros = (0,) * len(shape)
        if l == 0:
            return pl.BlockSpec((None, depth) + shape, lambda b: (b, 0) + zeros)
        return pl.BlockSpec((None, None) + shape, lambda b: (b, l) + zeros)

    def ret_ctx(proj, dec, states, l):
        col = lambda k: pl.BlockSpec((seq, BRANCH_W), lambda b: (b, k))
        state_shape = (2, N_RET_HEADS, RET_DK, RET_DV)
        carried = () if l == 0 else (states,)
        return pl.pallas_call(
            functools.partial(_ret_ctx_kernel, n_chunks=n_chunks_ctx, first_layer=l == 0),
            grid=(batch,),
            in_specs=[col(0), col(1), col(2), col(3),
                      _resident((None, 2, N_RET_HEADS, 4, RET_CHUNK, RET_CHUNK), lambda b: (l, 0, 0, 0, 0, 0))]
            + [hbm_spec] * len(carried),
            out_specs=[pl.BlockSpec((seq, BRANCH_W), lambda b: (b, 0)), stacked(state_shape, l)],
            out_shape=[jax.ShapeDtypeStruct((n_ctx, BRANCH_W), BF16),
                       jax.ShapeDtypeStruct((batch, depth) + state_shape, F32)],
            input_output_aliases={5: 1} if carried else {},
            compiler_params=_params(1),
            name="retention_ctx",
        )(proj, proj, proj, proj, dec, *carried)

    def ret_lat(proj, dec, l):
        proj4 = proj.reshape(n_tok // n_lat, dec_batch, dec_seq, proj_w)
        lat = n_ctx // n_lat

        n_steps = n_chunks_lat // RET_STEP
        step_rows = RET_STEP * RET_CHUNK

        def block_of(p, si):
            return jnp.where(p == 0, si, n_steps - 1 - si)

        col = lambda k: pl.BlockSpec((None, dec_batch, step_rows, BRANCH_W),
                                     lambda p, si: (lat, 0, block_of(p, si), k))
        return pl.pallas_call(
            functools.partial(_ret_lat_kernel, n_steps=n_steps, n_seqs=dec_batch),
            grid=(2, n_steps),
            in_specs=[col(0), col(1), col(2), col(3),
                      pl.BlockSpec((dec_batch, None, None, N_RET_HEADS, RET_DK, RET_DV),
                                   lambda p, si: (0, l, p, 0, 0, 0)),
                      pl.BlockSpec((None, None, N_RET_HEADS, 4, RET_CHUNK, RET_CHUNK),
                                   lambda p, si: (l, p, 0, 0, 0, 0))],
            out_specs=pl.BlockSpec((dec_batch, step_rows, BRANCH_W),
                                   lambda p, si: (0, n_steps - 1 - jnp.where(p == 0, 0, si), 0)),
            out_shape=jax.ShapeDtypeStruct((dec_batch, dec_seq, BRANCH_W), BF16),
            scratch_shapes=[pltpu.VMEM((dec_batch, N_RET_HEADS, RET_DK, RET_DV), F32),
                            pltpu.VMEM((dec_batch, dec_seq, BRANCH_W), F32)],
            compiler_params=_params(2),
            name="retention_lat",
        )(proj4, proj4, proj4, proj4, state_ret, dec).reshape(n_lat, BRANCH_W)

    def na_ctx(proj, ck, cv, l):
        col = lambda k: pl.BlockSpec((seq, BRANCH_W), lambda b: (b, k))
        slab = (N_NA_HEADS, seq, NA_HEAD_DIM)
        cache_shape = jax.ShapeDtypeStruct((batch, depth) + slab, F32)
        carried = () if l == 0 else (ck, cv)
        return pl.pallas_call(
            functools.partial(_ctx_attn_kernel, first_layer=l == 0),
            grid=(batch,),
            in_specs=[col(7), col(8), col(9)] + [hbm_spec] * len(carried),
            out_specs=[pl.BlockSpec((seq, BRANCH_W), lambda b: (b, 0)), stacked(slab, l), stacked(slab, l)],
            out_shape=[jax.ShapeDtypeStruct((n_ctx, BRANCH_W), BF16), cache_shape, cache_shape],
            input_output_aliases={3: 1, 4: 2} if carried else {},
            compiler_params=_params(1),
            name="attention_ctx",
        )(proj, proj, proj, *carried)

    def na_lat(proj, rpb_rows, l):
        q_rows = NA_SUB * NA_ROWS * GRID_W
        pairs = N_NA_HEADS // 2
        q_base = n_ctx // q_rows
        kv_base = n_ctx // dec_seq
        cache_spec = pl.BlockSpec((None, None, 2, past_len, NA_HEAD_DIM), lambda b, hp, i: (b, l, hp, 0, 0))
        return pl.pallas_call(
            functools.partial(_na_kernel, n_rows=n_rows),
            grid=(dec_batch, pairs, na_steps),
            in_specs=[pl.BlockSpec((q_rows, LANES), lambda b, hp, i: (q_base + b * na_steps + i, 7 * pairs + hp)),
                      pl.BlockSpec((dec_seq, LANES), lambda b, hp, i: (kv_base + b, 8 * pairs + hp)),
                      pl.BlockSpec((dec_seq, LANES), lambda b, hp, i: (kv_base + b, 9 * pairs + hp)),
                      cache_spec, cache_spec,
                      pl.BlockSpec((None, 2, 2 * NA_KH - 1, 2 * GRID_W), lambda b, hp, i: (l, hp, 0, 0))],
            out_specs=pl.BlockSpec((q_rows, LANES), lambda b, hp, i: (b * na_steps + i, hp)),
            out_shape=jax.ShapeDtypeStruct((n_lat, BRANCH_W), BF16),
            scratch_shapes=[pltpu.VMEM((2, 3, NA_ROWS * GRID_W, NA_WIN * GRID_W), F32)],
            compiler_params=_params(3),
            name="attention_lat",
        )(proj, proj, proj, cache_na_k, cache_na_v, rpb_rows)

    def mixout(xv, ub, proj, ret_c, ret_l, na_c, na_l, l):
        col = lambda k: pl.BlockSpec((tm, BRANCH_W), lambda i: (i, k))
        before = lambda k: pl.BlockSpec((HALO, BRANCH_W), lambda i: (jnp.maximum(i * (tm // HALO) - 1, 0), k))
        after = lambda k: pl.BlockSpec((HALO, BRANCH_W),
                                       lambda i: (jnp.minimum((i + 1) * (tm // HALO), n_tok // HALO - 1), k))
        ctx_rows = pl.BlockSpec((tm, BRANCH_W), lambda i: (jnp.minimum(i, ctx_blocks - 1), 0))
        lat_rows = pl.BlockSpec((tm, BRANCH_W), lambda i: (jnp.maximum(i - ctx_blocks, 0), 0))
        return pl.pallas_call(
            functools.partial(_mixout_kernel, tm=tm, n_ctx=n_ctx, seq=seq, dec_seq=dec_seq),
            grid=(n_blocks,),
            in_specs=[x_spec, x_spec, mod_spec(l), col(4), col(5), col(6), before(5), before(6), after(5),
                      after(6), ctx_rows, lat_rows, ctx_rows, lat_rows,
                      pl.BlockSpec((None, CONV_K, BRANCH_W), lambda i: (l, 0, 0)),
                      pl.BlockSpec((None, 1, BRANCH_W), lambda i: (l, 0, 0)),
                      _resident((None, d, N_BRANCH * d), lambda i: (l, 0, 0)),
                      pl.BlockSpec((None, 1, N_BRANCH * d), lambda i: (l, 0, 0)),
                      _resident((None, N_BRANCH, BRANCH_W, d), lambda i: (l, 0, 0, 0)),
                      _resident((None, d, d), lambda i: (l, 0, 0))],
            out_specs=x_spec,
            out_shape=x_shape,
            scratch_shapes=[pltpu.VMEM((tm, d), BF16)],
            compiler_params=_params(1),
            name="mixer_out",
        )(xv, ub, mod, proj, proj, proj, proj, proj, proj, proj, ret_c, ret_l, na_c, na_l, conv_w, conv_b3,
          w_merge_b, b_merge3, w_branch_b, w_out_b)

    dec_all = jax.vmap(_decay_tables)(ret_decay_logit)
    rpb_rows = _na_bias_rows(na_rpb)
    states = cache_k = cache_v = None
    for l in range(depth):
        x = ffn(x, l, 0)
        proj, ub = mixin(x, l)
        ret_c, states = ret_ctx(proj, dec_all, states, l)
        ret_l = ret_lat(proj, dec_all, l)
        na_c, cache_k, cache_v = na_ctx(proj, cache_k, cache_v, l)
        na_l = na_lat(proj, rpb_rows, l)
        x = mixout(x, ub, proj, ret_c, ret_l, na_c, na_l, l)
        x = ffn(x, l, 1, final=l == depth - 1)

    y_prompt, y_sample = x
    return (y_prompt.reshape(batch, seq, d), y_sample.reshape(dec_batch, dec_seq, d), states, cache_k, cache_v)
```

```python
import functools

import jax
import jax.numpy as jnp
import numpy as np
from jax import lax
from jax.experimental import pallas as pl
from jax.experimental.pallas import tpu as pltpu

D_MODEL = 1024
GRID_W = 64
N_RET_HEADS = 4
RET_DK = 128
RET_DV = 128
RET_CHUNK = 128
CONV_K = 3
N_NA_HEADS = 8
NA_HEAD_DIM = 64
NA_KH = 8
NA_KW = 16
BRANCH_W = 512
N_BRANCH = 3
N_SPLIT = 10
D_FF = 2816
N_MOD = 9
ROPE_BASE = 10000.0
EPS = 1e-6
NEG_INF = -1e30
LOG2E = 1.4426950408889634

BF16 = jnp.bfloat16
F32 = jnp.float32

VMEM_LIMIT_BYTES = 56 * 1024 * 1024
LANES = 128
COND_ROWS = 8
MOD_COLS = 2304
TOKEN_BLOCK = 512
FFN_BLOCK = 1024
FF_CHUNK = 256
RET_STEP = 4
NA_ROWS = 4
NA_WIN = NA_ROWS + NA_KH - 1
NA_SUB = 4
HALO = 8


def _params(n_axes):
    return pltpu.CompilerParams(dimension_semantics=("arbitrary",) * n_axes,
                                vmem_limit_bytes=VMEM_LIMIT_BYTES)


def _resident(block_shape, index_map):
    return pl.BlockSpec(block_shape, index_map, pipeline_mode=pl.Buffered(1))


def _dot(a, b):
    return jnp.dot(a, b, preferred_element_type=F32)


def _dot_nt(a, b):
    return lax.dot_general(a, b, (((1,), (1,)), ((), ())), preferred_element_type=F32)


def _norm_mod(x, g_row, scale_row, shift_row):
    ms = jnp.mean(x * x, axis=-1, keepdims=True)
    y = x * lax.rsqrt(ms + EPS)
    return (y * g_row) * (1.0 + scale_row) + shift_row


def _mod_kernel(cond_ref, w_ref, b_ref, o_ref):
    a = jax.nn.silu(cond_ref[...]).astype(BF16)
    o_ref[...] = _dot(a, w_ref[...].astype(BF16)) + b_ref[...]


def _mod_call(cond, w_mod, b_mod):
    depth, d, n = w_mod.shape
    tn = MOD_COLS
    return pl.pallas_call(
        _mod_kernel,
        grid=(depth, n // tn),
        in_specs=[pl.BlockSpec((COND_ROWS, d), lambda l, j: (0, 0)),
                  pl.BlockSpec((None, d, tn), lambda l, j: (l, 0, j)),
                  pl.BlockSpec((None, 1, tn), lambda l, j: (l, 0, j))],
        out_specs=pl.BlockSpec((None, COND_ROWS, tn), lambda l, j: (l, 0, j)),
        out_shape=jax.ShapeDtypeStruct((depth, COND_ROWS, n), F32),
        compiler_params=_params(2),
        name="mod_vectors",
    )(cond, w_mod, b_mod.reshape(depth, 1, n))


def _ffn_kernel(*refs, gi, mi, ctx_steps, split_in, final):
    refs = list(refs)
    x_refs = [refs.pop(0) for _ in range(2 if split_in else 1)]
    mod_ref, g_ref, w1_ref, w2_ref = (refs.pop(0) for _ in range(4))
    fg_ref = refs.pop(0) if final else None
    o_refs = [refs.pop(0) for _ in range(2 if final else 1)]
    hb_ref, t_ref = refs
    is_ctx = pl.program_id(0) < ctx_steps

    def rows():
        return jnp.where(is_ctx, x_refs[0][...], x_refs[1][...]) if split_in else x_refs[0][...]

    h = _norm_mod(rows(), g_ref[gi:gi + 1, :], mod_ref[mi + 1:mi + 2, :], mod_ref[mi:mi + 1, :])
    hb_ref[...] = h.astype(BF16)
    for j in range(D_FF // FF_CHUNK):
        hb = hb_ref[...]
        a = _dot(hb, w1_ref[:, j * FF_CHUNK:(j + 1) * FF_CHUNK])
        b = _dot(hb, w1_ref[:, D_FF + j * FF_CHUNK:D_FF + (j + 1) * FF_CHUNK])
        t_ref[:, j * FF_CHUNK:(j + 1) * FF_CHUNK] = (jax.nn.silu(a) * b).astype(BF16)
    x_new = rows() + (0.5 * mod_ref[mi + 2:mi + 3, :]) * _dot(t_ref[...], w2_ref[...])
    if not final:
        o_refs[0][...] = x_new
        return
    ms = jnp.mean(x_new * x_new, axis=-1, keepdims=True)
    y = (x_new * lax.rsqrt(ms + EPS)) * fg_ref[...]

    @pl.when(is_ctx)
    def _():
        o_refs[0][...] = y

    @pl.when(jnp.logical_not(is_ctx))
    def _():
        o_refs[1][...] = y


def _mixin_kernel(x_ref, mod_ref, g_ref, w_ref, cos_ref, sin_ref, o_ref, u_ref):
    u = _norm_mod(x_ref[...], g_ref[1:2, :], mod_ref[4:5, :], mod_ref[3:4, :]).astype(BF16)
    u_ref[...] = u
    cos = cos_ref[...]
    sin = sin_ref[...]
    lane = lax.broadcasted_iota(jnp.int32, cos.shape, 1)
    first = (lane % (RET_DK // 2)) < (RET_DK // 4)
    for k in range(N_SPLIT):
        sl = slice(k * BRANCH_W, (k + 1) * BRANCH_W)
        r = _dot(u, w_ref[:, sl])
        if k < 2:
            heads = []
            for h in range(N_RET_HEADS):
                xh = r[:, h * RET_DK:(h + 1) * RET_DK]
                if k == 0:
                    xh = xh * (RET_DK ** -0.5)
                swapped = jnp.where(first, pltpu.roll(xh, RET_DK - RET_DK // 4, 1), pltpu.roll(xh, RET_DK // 4, 1))
                heads.append(xh * cos + swapped * sin)
            r = jnp.concatenate(heads, axis=-1)
        o_ref[:, sl] = r


def _layer_slab(refs, first_layer):
    if not first_layer:
        return refs
    for r in refs:
        if r.shape[0] > 1:
            r[1:] = jnp.zeros((r.shape[0] - 1,) + r.shape[1:], r.dtype)
    return [r.at[0] for r in refs]


def _decayed_keys_t(k_tiles, zeta_tiles):
    kz = jnp.concatenate([k * z for k, z in zip(k_tiles, zeta_tiles)], axis=-1)
    kzt = kz.T.astype(BF16)
    return [kzt[i * RET_DK:(i + 1) * RET_DK] for i in range(len(k_tiles))]


def _ret_level(items):
    stage = []
    for q, k, kzt, v, s, dm, xi, gc in items:
        qb = q.astype(BF16)
        vb = v.astype(BF16)
        stage.append((vb, s, dm, xi, gc, _dot_nt(qb, k.astype(BF16)), _dot(qb, s.astype(BF16)), _dot(kzt, vb)))
    out = []
    for vb, s, dm, xi, gc, qk, qs, kv in stage:
        s_new = s * gc + kv
        out.append((_dot((qk * dm).astype(BF16), vb) + qs * xi, s_new))
    return out


def _head_norm_gate(os, gs):
    mus = [jnp.mean(o, axis=-1, keepdims=True) for o in os]
    ds = [o - mu for o, mu in zip(os, mus)]
    vs = [jnp.mean(jnp.square(d), axis=-1, keepdims=True) for d in ds]
    return [(d * lax.rsqrt(v + EPS)) * jax.nn.silu(g) for d, v, g in zip(ds, vs, gs)]


def _ret_ctx_kernel(q_ref, k_ref, v_ref, g_ref, dec_ref, *rest, n_chunks, first_layer):
    st_ref = _layer_slab(rest[-1:], first_layer)[0]
    o_ref = rest[-2]
    c = RET_CHUNK
    heads = [slice(h * RET_DK, (h + 1) * RET_DK) for h in range(N_RET_HEADS)]
    rows = [slice(ci * c, (ci + 1) * c) for ci in range(n_chunks)]
    chains = [(d, h) for d in range(2) for h in range(N_RET_HEADS)]
    state = {dh: jnp.zeros((RET_DK, RET_DV), F32) for dh in chains}
    o_tot = {}
    for t in range(n_chunks):
        cis = {(d, h): (t if d == 0 else n_chunks - 1 - t) for d, h in chains}
        kzt = _decayed_keys_t([k_ref[rows[cis[d, h]], heads[h]] for d, h in chains],
                              [dec_ref[d, h, 2] for d, h in chains])
        items = [(q_ref[rows[cis[d, h]], heads[h]], k_ref[rows[cis[d, h]], heads[h]], kzt_dh,
                  v_ref[rows[cis[d, h]], heads[h]], state[d, h], dec_ref[d, h, 0], dec_ref[d, h, 1], dec_ref[d, h, 3])
                 for (d, h), kzt_dh in zip(chains, kzt)]
        for (d, h), (o, s_new) in zip(chains, _ret_level(items)):
            state[d, h] = s_new
            key = (cis[d, h], h)
            o_tot[key] = o if key not in o_tot else o_tot[key] + o
    for d, h in chains:
        st_ref[d, h] = state[d, h]
    tiles = [(ci, h) for ci in range(n_chunks) for h in range(N_RET_HEADS)]
    normed = _head_norm_gate([o_tot[t] for t in tiles], [g_ref[rows[ci], heads[h]] for ci, h in tiles])
    for (ci, h), y in zip(tiles, normed):
        o_ref[rows[ci], heads[h]] = y.astype(o_ref.dtype)


def _ret_lat_kernel(q_ref, k_ref, v_ref, g_ref, s0_ref, dec_ref, o_ref, s_scr, of_scr, *, n_steps, n_seqs):
    p = pl.program_id(0)
    si = pl.program_id(1)

    @pl.when(si == 0)
    def _():
        s_scr[...] = s0_ref[...]

    block = jnp.where(p == 0, si, n_steps - 1 - si)
    heads = [slice(h * RET_DK, (h + 1) * RET_DK) for h in range(N_RET_HEADS)]
    tiles = [(b, h) for b in range(n_seqs) for h in range(N_RET_HEADS)]
    state = {bh: s_scr[bh] for bh in tiles}
    done = []
    for t in range(RET_STEP):
        first_row = pl.multiple_of(jnp.where(p == 0, t, RET_STEP - 1 - t) * RET_CHUNK, RET_CHUNK)
        rows = pl.ds(first_row, RET_CHUNK)
        kzt = _decayed_keys_t([k_ref[b, rows, heads[h]] for b, h in tiles], [dec_ref[h, 2] for _, h in tiles])
        items = [(q_ref[b, rows, heads[h]], k_ref[b, rows, heads[h]], kzt_bh, v_ref[b, rows, heads[h]], state[b, h],
                  dec_ref[h, 0], dec_ref[h, 1], dec_ref[h, 3]) for (b, h), kzt_bh in zip(tiles, kzt)]
        os = []
        for bh, (o, s_new) in zip(tiles, _ret_level(items)):
            state[bh] = s_new
            os.append(o)
        done.append((first_row, rows, os))
    for bh in tiles:
        s_scr[bh] = state[bh]

    def parked(first_row):
        return pl.ds(pl.multiple_of(block * (RET_STEP * RET_CHUNK) + first_row, RET_CHUNK), RET_CHUNK)

    @pl.when(p == 0)
    def _():
        for first_row, _, os in done:
            for (b, h), o in zip(tiles, os):
                of_scr[b, parked(first_row), heads[h]] = o

    @pl.when(p == 1)
    def _():
        for first_row, rows, os in done:
            tots = [of_scr[b, parked(first_row), heads[h]] + o for (b, h), o in zip(tiles, os)]
            normed = _head_norm_gate(tots, [g_ref[b, rows, heads[h]] for b, h in tiles])
            for (b, h), y in zip(tiles, normed):
                o_ref[b, rows, heads[h]] = y.astype(o_ref.dtype)


def _softmax_parts(parts):
    m = functools.reduce(jnp.maximum, [jnp.max(s, axis=-1, keepdims=True) for s in parts])
    es = [jnp.exp2(s - m) for s in parts]
    den = functools.reduce(lambda a, b: a + b, [jnp.sum(e, axis=-1, keepdims=True) for e in es])
    return es, 1.0 / den


def _split_pair(q2):
    low = lax.broadcasted_iota(jnp.int32, q2.shape, 1) < NA_HEAD_DIM
    return low, (jnp.where(low, q2, 0.0).astype(BF16), jnp.where(low, 0.0, q2).astype(BF16))


def _ctx_attn_kernel(q_ref, k_ref, v_ref, *rest, first_layer):
    o_ref = rest[-3]
    ck_ref, cv_ref = _layer_slab(rest[-2:], first_layer)
    pairs = [slice(hp * LANES, (hp + 1) * LANES) for hp in range(N_NA_HEADS // 2)]
    scores = []
    for hp, sl in enumerate(pairs):
        k2 = k_ref[:, sl]
        v2 = v_ref[:, sl]
        for hh in range(2):
            ck_ref[2 * hp + hh] = k2[:, hh * NA_HEAD_DIM:(hh + 1) * NA_HEAD_DIM]
            cv_ref[2 * hp + hh] = v2[:, hh * NA_HEAD_DIM:(hh + 1) * NA_HEAD_DIM]
        kb = k2.astype(BF16)
        low, q_heads = _split_pair(q_ref[:, sl] * (NA_HEAD_DIM ** -0.5 * LOG2E))
        scores += [_dot_nt(qh, kb) for qh in q_heads]
    weights = [_softmax_parts([s]) for s in scores]
    for hp, sl in enumerate(pairs):
        vb = v_ref[:, sl].astype(BF16)
        outs = []
        for hh in range(2):
            (e,), inv = weights[2 * hp + hh]
            outs.append(_dot(e.astype(BF16), vb) * inv)
        o_ref[:, sl] = jnp.where(low, outs[0], outs[1]).astype(o_ref.dtype)


def _na_kernel(q_ref, k_ref, v_ref, kc_ref, vc_ref, rpb_ref, o_ref, bias_ref, *, n_rows):
    i = pl.program_id(2)
    q_rows = NA_ROWS * GRID_W
    n_groups = n_rows // NA_ROWS

    @pl.when((pl.program_id(1) == 0) & (i == 0))
    def _():
        w = GRID_W
        masked = jnp.full((w, w), NEG_INF, F32)
        qc = lax.broadcasted_iota(jnp.int32, (w, w), 0)
        kc = lax.broadcasted_iota(jnp.int32, (w, w), 1)
        c0 = jnp.clip(qc - NA_KW // 2, 0, w - NA_KW)
        col_ok = (kc >= c0) & (kc < c0 + NA_KW)
        offsets = _na_row_offsets()
        for hh in range(2):
            tiles = {}
            for dr in sorted({int(o) for o in offsets.ravel() if o >= 0}):
                rows = jnp.broadcast_to(rpb_ref[hh, dr:dr + 1, :], (w, 2 * w))
                skewed = pltpu.roll(rows, w + 1, 1, stride=1, stride_axis=0)
                tiles[dr] = jnp.where(col_ok, skewed[:, :w], NEG_INF)
            for p in range(3):
                for r in range(NA_ROWS):
                    row = [masked if offsets[p, r, k] < 0 else tiles[int(offsets[p, r, k])] for k in range(NA_WIN)]
                    bias_ref[hh, p, r * w:(r + 1) * w, :] = jnp.concatenate(row, axis=-1)

    kcb = [kc_ref[hh].astype(BF16) for hh in range(2)]
    vcb = [vc_ref[hh].astype(BF16) for hh in range(2)]
    scores = []
    for j in range(NA_SUB):
        grp = i * NA_SUB + j
        pattern = jnp.where(grp == 0, 0, jnp.where(grp == n_groups - 1, 2, 1))
        win0 = jnp.clip(NA_ROWS * grp - NA_KH // 2, 0, n_rows - NA_WIN)
        start = pl.multiple_of(win0 * GRID_W, GRID_W)
        kw = k_ref[pl.ds(start, NA_WIN * GRID_W), :].astype(BF16)
        q2 = q_ref[j * q_rows:(j + 1) * q_rows, :] * (NA_HEAD_DIM ** -0.5 * LOG2E)
        low, q_heads = _split_pair(q2)
        for hh in range(2):
            s_loc = _dot_nt(q_heads[hh], kw) + bias_ref[hh, pattern]
            s_ctx = _dot_nt(q2[:, hh * NA_HEAD_DIM:(hh + 1) * NA_HEAD_DIM].astype(BF16), kcb[hh])
            scores.append((s_loc, s_ctx))
    weights = [_softmax_parts(list(sc)) for sc in scores]
    for j in range(NA_SUB):
        grp = i * NA_SUB + j
        win0 = jnp.clip(NA_ROWS * grp - NA_KH // 2, 0, n_rows - NA_WIN)
        start = pl.multiple_of(win0 * GRID_W, GRID_W)
        vw = v_ref[pl.ds(start, NA_WIN * GRID_W), :].astype(BF16)
        loc, ctx = [], []
        for hh in range(2):
            (e_loc, e_ctx), inv = weights[2 * j + hh]
            loc.append(_dot(e_loc.astype(BF16), vw) * inv)
            ctx.append(_dot(e_ctx.astype(BF16), vcb[hh]) * inv)
        low = lax.broadcasted_iota(jnp.int32, loc[0].shape, 1) < NA_HEAD_DIM
        o_ref[j * q_rows:(j + 1) * q_rows, :] = (jnp.where(low, loc[0], loc[1])
                                                 + jnp.concatenate(ctx, axis=-1)).astype(o_ref.dtype)


def _mixout_kernel(x_ref, u_ref, mod_ref, cb_ref, cc_ref, ch_ref, ccp_ref, chp_ref, ccn_ref, chn_ref,
                   retc_ref, retl_ref, nac_ref, nal_ref, cw_ref, cbias_ref, wm_ref, bm_ref, wb_ref, wo_ref,
                   o_ref, mb_ref, *, tm, n_ctx, seq, dec_seq):
    i = pl.program_id(0)
    is_ctx = i * tm < n_ctx
    ub = u_ref[...]

    z = cc_ref[...] * ch_ref[...]
    z_before = (ccp_ref[...] * chp_ref[...])[HALO - 1:HALO, :]
    z_after = (ccn_ref[...] * chn_ref[...])[0:1, :]
    row = lax.broadcasted_iota(jnp.int32, z.shape, 0)
    last_pos = jnp.where(is_ctx, seq - 1, dec_seq - 1)
    pos = (i * tm - jnp.where(is_ctx, 0, n_ctx) + row) & last_pos
    z_prev = jnp.where(row == 0, z_before, pltpu.roll(z, 1, 0))
    z_prev = jnp.where(pos == 0, 0.0, z_prev)
    z_next = jnp.where(row == tm - 1, z_after, pltpu.roll(z, tm - 1, 0))
    z_next = jnp.where(pos == last_pos, 0.0, z_next)
    y = z_prev * cw_ref[0:1, :] + z * cw_ref[1:2, :] + z_next * cw_ref[2:3, :] + cbias_ref[...]
    conv_out = cb_ref[...] * y

    ret = jnp.where(is_ctx, retc_ref[...], retl_ref[...])
    na = jnp.where(is_ctx, nac_ref[...], nal_ref[...])
    branches = (ret, conv_out.astype(BF16), na)
    for c in range(D_MODEL // FF_CHUNK):
        merged = None
        for b in range(N_BRANCH):
            sl = slice(b * D_MODEL + c * FF_CHUNK, b * D_MODEL + (c + 1) * FF_CHUNK)
            gate = jax.nn.sigmoid(_dot(ub, wm_ref[:, sl]) + bm_ref[:, sl])
            term = gate * _dot(branches[b], wb_ref[b, :, c * FF_CHUNK:(c + 1) * FF_CHUNK])
            merged = term if merged is None else merged + term
        mb_ref[:, c * FF_CHUNK:(c + 1) * FF_CHUNK] = merged.astype(BF16)
    o_ref[...] = x_ref[...] + mod_ref[5:6, :] * _dot(mb_ref[...], wo_ref[...])


def _rope_tables(length, dim):
    t = jnp.arange(length)
    half = dim // 2
    quarter = half // 2
    inv_freq = ROPE_BASE ** (-jnp.arange(quarter, dtype=F32) * 2.0 / half)

    def tables(pos):
        ang = pos.astype(F32)[:, None] * inv_freq[None, :]
        cos, sin = jnp.cos(ang), jnp.sin(ang)
        return jnp.concatenate([cos, cos], axis=-1), jnp.concatenate([-sin, sin], axis=-1)

    cr, sr = tables(t // GRID_W)
    cc, sc = tables(t % GRID_W)
    return jnp.concatenate([cr, cc], axis=-1), jnp.concatenate([sr, sc], axis=-1)


def _decay_tables(decay_logit):
    c = RET_CHUNK
    log_g = jax.nn.log_sigmoid(decay_logit.astype(F32))
    idx = jnp.arange(c, dtype=F32)
    diff = idx[:, None] - idx[None, :]
    out = []
    for d in range(2):
        lg = log_g[d]
        decay_in = jnp.where(diff >= 0, jnp.exp(lg[:, None, None] * jnp.maximum(diff, 0.0)), 0.0)
        xi = jnp.exp(lg[:, None] * (idx + 1.0))
        zeta = jnp.exp(lg[:, None] * (c - 1.0 - idx))
        g_chunk = jnp.exp(lg * c)
        if d == 1:
            decay_in = jnp.swapaxes(decay_in, 1, 2)
            xi = xi[:, ::-1]
            zeta = zeta[:, ::-1]
        full = (c, c)
        out.append(jnp.stack([decay_in,
                              jnp.broadcast_to(xi[:, :, None], (N_RET_HEADS,) + full),
                              jnp.broadcast_to(zeta[:, :, None], (N_RET_HEADS,) + full),
                              jnp.broadcast_to(g_chunk[:, None, None], (N_RET_HEADS,) + full)], axis=1))
    return jnp.stack(out, axis=0)


def _na_bias_rows(rpb):
    lead = GRID_W - NA_KW
    pad = [(0, 0)] * (rpb.ndim - 1) + [(lead, 2 * GRID_W - lead - rpb.shape[-1])]
    return jnp.pad(rpb.astype(F32) * LOG2E, pad, constant_values=NEG_INF)


def _na_row_offsets():
    rr = np.arange(NA_ROWS)
    kr = np.arange(NA_WIN)
    rel_r = np.stack([rr, rr + NA_KH // 2, rr + NA_WIN - NA_ROWS])
    rel_r0 = np.stack([np.zeros_like(rr), rr, np.full_like(rr, NA_WIN - NA_KH)])
    row_ok = (kr[None, None, :] >= rel_r0[:, :, None]) & (kr[None, None, :] < rel_r0[:, :, None] + NA_KH)
    return np.where(row_ok, kr[None, None, :] - rel_r[:, :, None] + (NA_KH - 1), -1)


def kernel(x_prompt, x_sample, c, state_ret, cache_na_k, cache_na_v, c_ctx, norm_g, w_mod, b_mod, ffn_w1, ffn_w2,
           w_in, ret_decay_logit, conv_w, conv_b, na_rpb, w_branch, w_merge, b_merge, w_out, final_g):
    batch, seq, d = x_prompt.shape
    dec_batch, dec_seq, _ = x_sample.shape
    depth = norm_g.shape[0]
    past_len = cache_na_k.shape[3]
    n_ctx = batch * seq
    n_lat = dec_batch * dec_seq
    n_tok = n_ctx + n_lat
    tm = TOKEN_BLOCK
    tf = FFN_BLOCK
    n_rows = dec_seq // GRID_W
    assert d == D_MODEL and 1 + dec_batch <= COND_ROWS
    assert n_ctx % tm == 0 and dec_seq % tm == 0 and tm % seq == 0 and n_ctx % tf == 0 and dec_seq % tf == 0
    assert seq & (seq - 1) == 0 and dec_seq & (dec_seq - 1) == 0
    assert seq % RET_CHUNK == 0 and dec_seq % (RET_STEP * RET_CHUNK) == 0 and n_ctx % n_lat == 0
    assert n_rows % (NA_ROWS * NA_SUB) == 0 and n_rows >= NA_WIN and D_FF % FF_CHUNK == 0
    n_blocks = n_tok // tm
    ctx_blocks = n_ctx // tm

    x = (x_prompt.reshape(n_ctx, d), x_sample.reshape(n_lat, d))
    cond = jnp.zeros((COND_ROWS, d), F32).at[0].set(c_ctx).at[1:1 + dec_batch].set(c)
    mod = _mod_call(cond, w_mod, b_mod).reshape(depth, COND_ROWS, N_MOD, d)

    w1_b = ffn_w1.astype(BF16)
    w2_b = ffn_w2.astype(BF16)
    w_in_b = w_in.astype(BF16)
    w_merge_b = w_merge.astype(BF16)
    w_branch_b = w_branch.astype(BF16)
    w_out_b = w_out.astype(BF16)
    b_merge3 = b_merge.reshape(depth, 1, N_BRANCH * d)
    conv_b3 = conv_b.reshape(depth, 1, BRANCH_W)
    rope_cos, rope_sin = _rope_tables(dec_seq, RET_DK)
    rope_cos = jnp.concatenate([jnp.ones((tm, RET_DK), F32), rope_cos], axis=0)
    rope_sin = jnp.concatenate([jnp.zeros((tm, RET_DK), F32), rope_sin], axis=0)
    n_chunks_lat = dec_seq // RET_CHUNK
    n_chunks_ctx = seq // RET_CHUNK
    na_steps = n_rows // (NA_ROWS * NA_SUB)
    proj_w = N_SPLIT * BRANCH_W

    def mod_spec(l, rows=tm):
        def index(i):
            row = i * rows
            return (l, jnp.where(row < n_ctx, 0, 1 + (row - n_ctx) // dec_seq), 0, 0)
        return pl.BlockSpec((None, None, N_MOD, d), index)

    def g_spec(l):
        return pl.BlockSpec((None, 3, d), lambda i: (l, 0, 0))

    x_spec = pl.BlockSpec((tm, d), lambda i: (i, 0))
    x_shape = jax.ShapeDtypeStruct((n_tok, d), F32)
    hbm_spec = pl.BlockSpec(memory_space=pl.ANY)

    def ffn(xv, l, s, final=False):
        ctx_steps = n_ctx // tf
        rows_spec = pl.BlockSpec((tf, d), lambda i: (i, 0))
        ctx_spec = pl.BlockSpec((tf, d), lambda i: (jnp.minimum(i, ctx_steps - 1), 0))
        lat_spec = pl.BlockSpec((tf, d), lambda i: (jnp.maximum(i - ctx_steps, 0), 0))
        split_in = isinstance(xv, tuple)
        x_in = list(xv) if split_in else [xv]
        extra_in = [final_g.reshape(1, d)] if final else []
        if final:
            out_specs = [ctx_spec, lat_spec]
            out_shape = [jax.ShapeDtypeStruct((n_ctx, d), F32), jax.ShapeDtypeStruct((n_lat, d), F32)]
        else:
            out_specs, out_shape = rows_spec, x_shape
        return pl.pallas_call(
            functools.partial(_ffn_kernel, gi=2 * s, mi=6 * s, ctx_steps=ctx_steps, split_in=split_in, final=final),
            grid=(n_tok // tf,),
            in_specs=([ctx_spec, lat_spec] if split_in else [rows_spec])
            + [mod_spec(l, tf), g_spec(l),
               _resident((None, None, d, 2 * D_FF), lambda i: (l, s, 0, 0)),
               _resident((None, None, D_FF, d), lambda i: (l, s, 0, 0))]
            + [pl.BlockSpec((1, d), lambda i: (0, 0))] * len(extra_in),
            out_specs=out_specs,
            out_shape=out_shape,
            scratch_shapes=[pltpu.VMEM((tf, d), BF16), pltpu.VMEM((tf, D_FF), BF16)],
            compiler_params=_params(1),
            name="ffn",
        )(*x_in, mod, norm_g, w1_b, w2_b, *extra_in)

    def mixin(xv, l):
        def table_block(i):
            row = i * tm
            return (jnp.where(row < n_ctx, 0, 1 + ((row - n_ctx) % dec_seq) // tm), 0)

        tab = pl.BlockSpec((tm, RET_DK), table_block)
        return pl.pallas_call(
            _mixin_kernel,
            grid=(n_blocks,),
            in_specs=[x_spec, mod_spec(l), g_spec(l), _resident((None, d, proj_w), lambda i: (l, 0, 0)), tab, tab],
            out_specs=[pl.BlockSpec((tm, proj_w), lambda i: (i, 0)), x_spec],
            out_shape=[jax.ShapeDtypeStruct((n_tok, proj_w), F32), jax.ShapeDtypeStruct((n_tok, d), BF16)],
            compiler_params=_params(1),
            name="mixer_in",
        )(xv, mod, norm_g, w_in_b, rope_cos, rope_sin)

    def stacked(shape, l):
        zeros = (0,) * len(shape)
        if l == 0:
            return pl.BlockSpec((None, depth) + shape, lambda b: (b, 0) + zeros)
        return pl.BlockSpec((None, None) + shape, lambda b: (b, l) + zeros)

    def ret_ctx(proj, dec, states, l):
        col = lambda k: pl.BlockSpec((seq, BRANCH_W), lambda b: (b, k))
        state_shape = (2, N_RET_HEADS, RET_DK, RET_DV)
        carried = () if l == 0 else (states,)
        return pl.pallas_call(
            functools.partial(_ret_ctx_kernel, n_chunks=n_chunks_ctx, first_layer=l == 0),
            grid=(batch,),
            in_specs=[col(0), col(1), col(2), col(3),
                      _resident((None, 2, N_RET_HEADS, 4, RET_CHUNK, RET_CHUNK), lambda b: (l, 0, 0, 0, 0, 0))]
            + [hbm_spec] * len(carried),
            out_specs=[pl.BlockSpec((seq, BRANCH_W), lambda b: (b, 0)), stacked(state_shape, l)],
            out_shape=[jax.ShapeDtypeStruct((n_ctx, BRANCH_W), BF16),
                       jax.ShapeDtypeStruct((batch, depth) + state_shape, F32)],
            input_output_aliases={5: 1} if carried else {},
            compiler_params=_params(1),
            name="retention_ctx",
        )(proj, proj, proj, proj, dec, *carried)

    def ret_lat(proj, dec, l):
        proj4 = proj.reshape(n_tok // n_lat, dec_batch, dec_seq, proj_w)
        lat = n_ctx // n_lat

        n_steps = n_chunks_lat // RET_STEP
        step_rows = RET_STEP * RET_CHUNK

        def block_of(p, si):
            return jnp.where(p == 0, si, n_steps - 1 - si)

        col = lambda k: pl.BlockSpec((None, dec_batch, step_rows, BRANCH_W),
                                     lambda p, si: (lat, 0, block_of(p, si), k))
        return pl.pallas_call(
            functools.partial(_ret_lat_kernel, n_steps=n_steps, n_seqs=dec_batch),
            grid=(2, n_steps),
            in_specs=[col(0), col(1), col(2), col(3),
                      pl.BlockSpec((dec_batch, None, None, N_RET_HEADS, RET_DK, RET_DV),
                                   lambda p, si: (0, l, p, 0, 0, 0)),
                      pl.BlockSpec((None, None, N_RET_HEADS, 4, RET_CHUNK, RET_CHUNK),
                                   lambda p, si: (l, p, 0, 0, 0, 0))],
            out_specs=pl.BlockSpec((dec_batch, step_rows, BRANCH_W),
                                   lambda p, si: (0, n_steps - 1 - jnp.where(p == 0, 0, si), 0)),
            out_shape=jax.ShapeDtypeStruct((dec_batch, dec_seq, BRANCH_W), BF16),
            scratch_shapes=[pltpu.VMEM((dec_batch, N_RET_HEADS, RET_DK, RET_DV), F32),
                            pltpu.VMEM((dec_batch, dec_seq, BRANCH_W), F32)],
            compiler_params=_params(2),
            name="retention_lat",
        )(proj4, proj4, proj4, proj4, state_ret, dec).reshape(n_lat, BRANCH_W)

    def na_ctx(proj, ck, cv, l):
        col = lambda k: pl.BlockSpec((seq, BRANCH_W), lambda b: (b, k))
        slab = (N_NA_HEADS, seq, NA_HEAD_DIM)
        cache_shape = jax.ShapeDtypeStruct((batch, depth) + slab, F32)
        carried = () if l == 0 else (ck, cv)
        return pl.pallas_call(
            functools.partial(_ctx_attn_kernel, first_layer=l == 0),
            grid=(batch,),
            in_specs=[col(7), col(8), col(9)] + [hbm_spec] * len(carried),
            out_specs=[pl.BlockSpec((seq, BRANCH_W), lambda b: (b, 0)), stacked(slab, l), stacked(slab, l)],
            out_shape=[jax.ShapeDtypeStruct((n_ctx, BRANCH_W), BF16), cache_shape, cache_shape],
            input_output_aliases={3: 1, 4: 2} if carried else {},
            compiler_params=_params(1),
            name="attention_ctx",
        )(proj, proj, proj, *carried)

    def na_lat(proj, rpb_rows, l):
        q_rows = NA_SUB * NA_ROWS * GRID_W
        pairs = N_NA_HEADS // 2
        q_base = n_ctx // q_rows
        kv_base = n_ctx // dec_seq
        cache_spec = pl.BlockSpec((None, None, 2, past_len, NA_HEAD_DIM), lambda hp, b, i: (b, l, hp, 0, 0))
        return pl.pallas_call(
            functools.partial(_na_kernel, n_rows=n_rows),
            grid=(pairs, dec_batch, na_steps),
            in_specs=[pl.BlockSpec((q_rows, LANES), lambda hp, b, i: (q_base + b * na_steps + i, 7 * pairs + hp)),
                      pl.BlockSpec((dec_seq, LANES), lambda hp, b, i: (kv_base + b, 8 * pairs + hp)),
                      pl.BlockSpec((dec_seq, LANES), lambda hp, b, i: (kv_base + b, 9 * pairs + hp)),
                      cache_spec, cache_spec,
                      pl.BlockSpec((None, 2, 2 * NA_KH - 1, 2 * GRID_W), lambda hp, b, i: (l, hp, 0, 0))],
            out_specs=pl.BlockSpec((q_rows, LANES), lambda hp, b, i: (b * na_steps + i, hp)),
            out_shape=jax.ShapeDtypeStruct((n_lat, BRANCH_W), BF16),
            scratch_shapes=[pltpu.VMEM((2, 3, NA_ROWS * GRID_W, NA_WIN * GRID_W), F32)],
            compiler_params=_params(3),
            name="attention_lat",
        )(proj, proj, proj, cache_na_k, cache_na_v, rpb_rows)

    def mixout(xv, ub, proj, ret_c, ret_l, na_c, na_l, l):
        col = lambda k: pl.BlockSpec((tm, BRANCH_W), lambda i: (i, k))
        before = lambda k: pl.BlockSpec((HALO, BRANCH_W), lambda i: (jnp.maximum(i * (tm // HALO) - 1, 0), k))
        after = lambda k: pl.BlockSpec((HALO, BRANCH_W),
                                       lambda i: (jnp.minimum((i + 1) * (tm // HALO), n_tok // HALO - 1), k))
        ctx_rows = pl.BlockSpec((tm, BRANCH_W), lambda i: (jnp.minimum(i, ctx_blocks - 1), 0))
        lat_rows = pl.BlockSpec((tm, BRANCH_W), lambda i: (jnp.maximum(i - ctx_blocks, 0), 0))
        return pl.pallas_call(
            functools.partial(_mixout_kernel, tm=tm, n_ctx=n_ctx, seq=seq, dec_seq=dec_seq),
            grid=(n_blocks,),
            in_specs=[x_spec, x_spec, mod_spec(l), col(4), col(5), col(6), before(5), before(6), after(5),
                      after(6), ctx_rows, lat_rows, ctx_rows, lat_rows,
                      pl.BlockSpec((None, CONV_K, BRANCH_W), lambda i: (l, 0, 0)),
                      pl.BlockSpec((None, 1, BRANCH_W), lambda i: (l, 0, 0)),
                      _resident((None, d, N_BRANCH * d), lambda i: (l, 0, 0)),
                      pl.BlockSpec((None, 1, N_BRANCH * d), lambda i: (l, 0, 0)),
                      _resident((None, N_BRANCH, BRANCH_W, d), lambda i: (l, 0, 0, 0)),
                      _resident((None, d, d), lambda i: (l, 0, 0))],
            out_specs=x_spec,
            out_shape=x_shape,
            scratch_shapes=[pltpu.VMEM((tm, d), BF16)],
            compiler_params=_params(1),
            name="mixer_out",
        )(xv, ub, mod, proj, proj, proj, proj, proj, proj, proj, ret_c, ret_l, na_c, na_l, conv_w, conv_b3,
          w_merge_b, b_merge3, w_branch_b, w_out_b)

    dec_all = jax.vmap(_decay_tables)(ret_decay_logit)
    rpb_rows = _na_bias_rows(na_rpb)
    states = cache_k = cache_v = None
    for l in range(depth):
        x = ffn(x, l, 0)
        proj, ub = mixin(x, l)
        ret_c, states = ret_ctx(proj, dec_all, states, l)
        ret_l = ret_lat(proj, dec_all, l)
        na_c, cache_k, cache_v = na_ctx(proj, cache_k, cache_v, l)
        na_l = na_lat(proj, rpb_rows, l)
        x = mixout(x, ub, proj, ret_c, ret_l, na_c, na_l, l)
        x = ffn(x, l, 1, final=l == depth - 1)

    y_prompt, y_sample = x
    return (y_prompt.reshape(batch, seq, d), y_sample.reshape(dec_batch, dec_seq, d), states, cache_k, cache_v)
```

```python
import functools

import jax
import jax.numpy as jnp
import numpy as np
from jax import lax
from jax.experimental import pallas as pl
from jax.experimental.pallas import tpu as pltpu

D_MODEL = 1024
GRID_W = 64
N_RET_HEADS = 4
RET_DK = 128
RET_DV = 128
RET_CHUNK = 128
CONV_K = 3
N_NA_HEADS = 8
NA_HEAD_DIM = 64
NA_KH = 8
NA_KW = 16
BRANCH_W = 512
N_BRANCH = 3
N_SPLIT = 10
D_FF = 2816
N_MOD = 9
ROPE_BASE = 10000.0
EPS = 1e-6
NEG_INF = -1e30
LOG2E = 1.4426950408889634

BF16 = jnp.bfloat16
F32 = jnp.float32

VMEM_LIMIT_BYTES = 56 * 1024 * 1024
LANES = 128
COND_ROWS = 8
MOD_COLS = 2304
TOKEN_BLOCK = 512
FFN_BLOCK = 1024
FF_CHUNK = 256
RET_STEP = 4
CTX_SEQS = 2
NA_ROWS = 4
NA_WIN = NA_ROWS + NA_KH - 1
NA_SUB = 4
HALO = 8


def _params(n_axes):
    return pltpu.CompilerParams(dimension_semantics=("arbitrary",) * n_axes,
                                vmem_limit_bytes=VMEM_LIMIT_BYTES)


def _resident(block_shape, index_map):
    return pl.BlockSpec(block_shape, index_map, pipeline_mode=pl.Buffered(1))


def _dot(a, b):
    return jnp.dot(a, b, preferred_element_type=F32)


def _dot_nt(a, b):
    return lax.dot_general(a, b, (((1,), (1,)), ((), ())), preferred_element_type=F32)


def _norm_mod(x, g_row, scale_row, shift_row):
    ms = jnp.mean(x * x, axis=-1, keepdims=True)
    y = x * lax.rsqrt(ms + EPS)
    return (y * g_row) * (1.0 + scale_row) + shift_row


def _mod_kernel(cond_ref, w_ref, b_ref, o_ref):
    a = jax.nn.silu(cond_ref[...]).astype(BF16)
    o_ref[...] = _dot(a, w_ref[...].astype(BF16)) + b_ref[...]


def _mod_call(cond, w_mod, b_mod):
    depth, d, n = w_mod.shape
    tn = MOD_COLS
    return pl.pallas_call(
        _mod_kernel,
        grid=(depth, n // tn),
        in_specs=[pl.BlockSpec((COND_ROWS, d), lambda l, j: (0, 0)),
                  pl.BlockSpec((None, d, tn), lambda l, j: (l, 0, j)),
                  pl.BlockSpec((None, 1, tn), lambda l, j: (l, 0, j))],
        out_specs=pl.BlockSpec((None, COND_ROWS, tn), lambda l, j: (l, 0, j)),
        out_shape=jax.ShapeDtypeStruct((depth, COND_ROWS, n), F32),
        compiler_params=_params(2),
        name="mod_vectors",
    )(cond, w_mod, b_mod.reshape(depth, 1, n))


def _ffn_kernel(*refs, gi, mi, ctx_steps, split_in, final):
    refs = list(refs)
    x_refs = [refs.pop(0) for _ in range(2 if split_in else 1)]
    mod_ref, g_ref, w1_ref, w2_ref = (refs.pop(0) for _ in range(4))
    fg_ref = refs.pop(0) if final else None
    o_refs = [refs.pop(0) for _ in range(2 if final else 1)]
    hb_ref, t_ref = refs
    is_ctx = pl.program_id(0) < ctx_steps

    def rows():
        return jnp.where(is_ctx, x_refs[0][...], x_refs[1][...]) if split_in else x_refs[0][...]

    h = _norm_mod(rows(), g_ref[gi:gi + 1, :], mod_ref[mi + 1:mi + 2, :], mod_ref[mi:mi + 1, :])
    hb_ref[...] = h.astype(BF16)
    for j in range(D_FF // FF_CHUNK):
        hb = hb_ref[...]
        a = _dot(hb, w1_ref[:, j * FF_CHUNK:(j + 1) * FF_CHUNK])
        b = _dot(hb, w1_ref[:, D_FF + j * FF_CHUNK:D_FF + (j + 1) * FF_CHUNK])
        t_ref[:, j * FF_CHUNK:(j + 1) * FF_CHUNK] = (jax.nn.silu(a) * b).astype(BF16)
    x_new = rows() + (0.5 * mod_ref[mi + 2:mi + 3, :]) * _dot(t_ref[...], w2_ref[...])
    if not final:
        o_refs[0][...] = x_new
        return
    ms = jnp.mean(x_new * x_new, axis=-1, keepdims=True)
    y = (x_new * lax.rsqrt(ms + EPS)) * fg_ref[...]

    @pl.when(is_ctx)
    def _():
        o_refs[0][...] = y

    @pl.when(jnp.logical_not(is_ctx))
    def _():
        o_refs[1][...] = y


def _mixin_kernel(x_ref, mod_ref, g_ref, w_ref, cos_ref, sin_ref, o_ref, u_ref):
    u = _norm_mod(x_ref[...], g_ref[1:2, :], mod_ref[4:5, :], mod_ref[3:4, :]).astype(BF16)
    u_ref[...] = u
    cos = cos_ref[...]
    sin = sin_ref[...]
    lane = lax.broadcasted_iota(jnp.int32, cos.shape, 1)
    first = (lane % (RET_DK // 2)) < (RET_DK // 4)
    for k in range(N_SPLIT):
        sl = slice(k * BRANCH_W, (k + 1) * BRANCH_W)
        r = _dot(u, w_ref[:, sl])
        if k < 2:
            heads = []
            for h in range(N_RET_HEADS):
                xh = r[:, h * RET_DK:(h + 1) * RET_DK]
                if k == 0:
                    xh = xh * (RET_DK ** -0.5)
                swapped = jnp.where(first, pltpu.roll(xh, RET_DK - RET_DK // 4, 1), pltpu.roll(xh, RET_DK // 4, 1))
                heads.append(xh * cos + swapped * sin)
            r = jnp.concatenate(heads, axis=-1)
        o_ref[:, sl] = r


def _layer_slab(refs, first_layer):
    if not first_layer:
        return refs
    for r in refs:
        if r.shape[1] > 1:
            r[:, 1:] = jnp.zeros((r.shape[0], r.shape[1] - 1) + r.shape[2:], r.dtype)
    return [r.at[:, 0] for r in refs]


def _decayed_keys_t(k_tiles, zeta_tiles):
    kz = jnp.concatenate([k * z for k, z in zip(k_tiles, zeta_tiles)], axis=-1)
    kzt = kz.T.astype(BF16)
    return [kzt[i * RET_DK:(i + 1) * RET_DK] for i in range(len(k_tiles))]


def _ret_level(items):
    stage = []
    for q, k, kzt, v, s, dm, xi, gc in items:
        qb = q.astype(BF16)
        vb = v.astype(BF16)
        stage.append((vb, s, dm, xi, gc, _dot_nt(qb, k.astype(BF16)), _dot(qb, s.astype(BF16)), _dot(kzt, vb)))
    out = []
    for vb, s, dm, xi, gc, qk, qs, kv in stage:
        s_new = s * gc + kv
        out.append((_dot((qk * dm).astype(BF16), vb) + qs * xi, s_new))
    return out


def _head_norm_gate(os, gs):
    mus = [jnp.mean(o, axis=-1, keepdims=True) for o in os]
    ds = [o - mu for o, mu in zip(os, mus)]
    vs = [jnp.mean(jnp.square(d), axis=-1, keepdims=True) for d in ds]
    return [(d * lax.rsqrt(v + EPS)) * jax.nn.silu(g) for d, v, g in zip(ds, vs, gs)]


def _ret_ctx_kernel(q_ref, k_ref, v_ref, g_ref, dec_ref, *rest, n_chunks, first_layer):
    st_ref = _layer_slab(rest[-1:], first_layer)[0]
    o_ref = rest[-2]
    c = RET_CHUNK
    heads = [slice(h * RET_DK, (h + 1) * RET_DK) for h in range(N_RET_HEADS)]
    rows = {(s, ci): slice((s * n_chunks + ci) * c, (s * n_chunks + ci + 1) * c)
            for s in range(CTX_SEQS) for ci in range(n_chunks)}
    chains = [(s, d, h) for s in range(CTX_SEQS) for d in range(2) for h in range(N_RET_HEADS)]
    state = {ch: jnp.zeros((RET_DK, RET_DV), F32) for ch in chains}
    o_tot = {}
    for t in range(n_chunks):
        at = {(s, d, h): rows[s, t if d == 0 else n_chunks - 1 - t] for s, d, h in chains}
        kzt = _decayed_keys_t([k_ref[at[ch], heads[ch[2]]] for ch in chains], [dec_ref[d, h, 2] for _, d, h in chains])
        items = [(q_ref[at[ch], heads[ch[2]]], k_ref[at[ch], heads[ch[2]]], kzt_ch, v_ref[at[ch], heads[ch[2]]],
                  state[ch], dec_ref[ch[1], ch[2], 0], dec_ref[ch[1], ch[2], 1], dec_ref[ch[1], ch[2], 3])
                 for ch, kzt_ch in zip(chains, kzt)]
        for ch, (o, s_new) in zip(chains, _ret_level(items)):
            state[ch] = s_new
            key = (at[ch].start, ch[2])
            o_tot[key] = o if key not in o_tot else o_tot[key] + o
    for s, d, h in chains:
        st_ref[s, d, h] = state[s, d, h]
    tiles = [(r, h) for r in rows.values() for h in range(N_RET_HEADS)]
    normed = _head_norm_gate([o_tot[r.start, h] for r, h in tiles], [g_ref[r, heads[h]] for r, h in tiles])
    for (r, h), y in zip(tiles, normed):
        o_ref[r, heads[h]] = y.astype(o_ref.dtype)


def _ret_lat_kernel(q_ref, k_ref, v_ref, g_ref, s0_ref, dec_ref, o_ref, s_scr, of_scr, *, n_steps, n_seqs):
    p = pl.program_id(0)
    si = pl.program_id(1)

    @pl.when(si == 0)
    def _():
        s_scr[...] = s0_ref[...]

    block = jnp.where(p == 0, si, n_steps - 1 - si)
    heads = [slice(h * RET_DK, (h + 1) * RET_DK) for h in range(N_RET_HEADS)]
    tiles = [(b, h) for b in range(n_seqs) for h in range(N_RET_HEADS)]
    state = {bh: s_scr[bh] for bh in tiles}
    done = []
    for t in range(RET_STEP):
        first_row = pl.multiple_of(jnp.where(p == 0, t, RET_STEP - 1 - t) * RET_CHUNK, RET_CHUNK)
        rows = pl.ds(first_row, RET_CHUNK)
        kzt = _decayed_keys_t([k_ref[b, rows, heads[h]] for b, h in tiles], [dec_ref[h, 2] for _, h in tiles])
        items = [(q_ref[b, rows, heads[h]], k_ref[b, rows, heads[h]], kzt_bh, v_ref[b, rows, heads[h]], state[b, h],
                  dec_ref[h, 0], dec_ref[h, 1], dec_ref[h, 3]) for (b, h), kzt_bh in zip(tiles, kzt)]
        os = []
        for bh, (o, s_new) in zip(tiles, _ret_level(items)):
            state[bh] = s_new
            os.append(o)
        done.append((first_row, rows, os))
    for bh in tiles:
        s_scr[bh] = state[bh]

    def parked(first_row):
        return pl.ds(pl.multiple_of(block * (RET_STEP * RET_CHUNK) + first_row, RET_CHUNK), RET_CHUNK)

    @pl.when(p == 0)
    def _():
        for first_row, _, os in done:
            for (b, h), o in zip(tiles, os):
                of_scr[b, parked(first_row), heads[h]] = o

    @pl.when(p == 1)
    def _():
        for first_row, rows, os in done:
            tots = [of_scr[b, parked(first_row), heads[h]] + o for (b, h), o in zip(tiles, os)]
            normed = _head_norm_gate(tots, [g_ref[b, rows, heads[h]] for b, h in tiles])
            for (b, h), y in zip(tiles, normed):
                o_ref[b, rows, heads[h]] = y.astype(o_ref.dtype)


def _softmax_parts(parts):
    m = functools.reduce(jnp.maximum, [jnp.max(s, axis=-1, keepdims=True) for s in parts])
    es = [jnp.exp2(s - m) for s in parts]
    den = functools.reduce(lambda a, b: a + b, [jnp.sum(e, axis=-1, keepdims=True) for e in es])
    return es, 1.0 / den


def _split_pair(q2):
    low = lax.broadcasted_iota(jnp.int32, q2.shape, 1) < NA_HEAD_DIM
    return low, (jnp.where(low, q2, 0.0).astype(BF16), jnp.where(low, 0.0, q2).astype(BF16))


def _ctx_attn_kernel(q_ref, k_ref, v_ref, *rest, first_layer):
    o_ref = rest[-3]
    ck_ref, cv_ref = _layer_slab(rest[-2:], first_layer)
    seq = q_ref.shape[0] // CTX_SEQS
    tiles = [(s, hp, slice(s * seq, (s + 1) * seq), slice(hp * LANES, (hp + 1) * LANES))
             for s in range(CTX_SEQS) for hp in range(N_NA_HEADS // 2)]
    scores = []
    for s, hp, rows, sl in tiles:
        k2 = k_ref[rows, sl]
        v2 = v_ref[rows, sl]
        for hh in range(2):
            ck_ref[s, 2 * hp + hh] = k2[:, hh * NA_HEAD_DIM:(hh + 1) * NA_HEAD_DIM]
            cv_ref[s, 2 * hp + hh] = v2[:, hh * NA_HEAD_DIM:(hh + 1) * NA_HEAD_DIM]
        kb = k2.astype(BF16)
        low, q_heads = _split_pair(q_ref[rows, sl] * (NA_HEAD_DIM ** -0.5 * LOG2E))
        scores += [_dot_nt(qh, kb) for qh in q_heads]
    weights = [_softmax_parts([sc]) for sc in scores]
    for t, (s, hp, rows, sl) in enumerate(tiles):
        vb = v_ref[rows, sl].astype(BF16)
        outs = []
        for hh in range(2):
            (e,), inv = weights[2 * t + hh]
            outs.append(_dot(e.astype(BF16), vb) * inv)
        o_ref[rows, sl] = jnp.where(low, outs[0], outs[1]).astype(o_ref.dtype)


def _na_kernel(q_ref, k_ref, v_ref, kc_ref, vc_ref, rpb_ref, o_ref, bias_ref, *, n_rows):
    i = pl.program_id(2)
    q_rows = NA_ROWS * GRID_W
    n_groups = n_rows // NA_ROWS

    @pl.when((pl.program_id(1) == 0) & (i == 0))
    def _():
        w = GRID_W
        masked = jnp.full((w, w), NEG_INF, F32)
        qc = lax.broadcasted_iota(jnp.int32, (w, w), 0)
        kc = lax.broadcasted_iota(jnp.int32, (w, w), 1)
        c0 = jnp.clip(qc - NA_KW // 2, 0, w - NA_KW)
        col_ok = (kc >= c0) & (kc < c0 + NA_KW)
        offsets = _na_row_offsets()
        for hh in range(2):
            tiles = {}
            for dr in sorted({int(o) for o in offsets.ravel() if o >= 0}):
                rows = jnp.broadcast_to(rpb_ref[hh, dr:dr + 1, :], (w, 2 * w))
                skewed = pltpu.roll(rows, w + 1, 1, stride=1, stride_axis=0)
                tiles[dr] = jnp.where(col_ok, skewed[:, :w], NEG_INF)
            for p in range(3):
                for r in range(NA_ROWS):
                    row = [masked if offsets[p, r, k] < 0 else tiles[int(offsets[p, r, k])] for k in range(NA_WIN)]
                    bias_ref[hh, p, r * w:(r + 1) * w, :] = jnp.concatenate(row, axis=-1)

    kcb = [kc_ref[hh].astype(BF16) for hh in range(2)]
    vcb = [vc_ref[hh].astype(BF16) for hh in range(2)]
    scores = []
    for j in range(NA_SUB):
        grp = i * NA_SUB + j
        pattern = jnp.where(grp == 0, 0, jnp.where(grp == n_groups - 1, 2, 1))
        win0 = jnp.clip(NA_ROWS * grp - NA_KH // 2, 0, n_rows - NA_WIN)
        start = pl.multiple_of(win0 * GRID_W, GRID_W)
        kw = k_ref[pl.ds(start, NA_WIN * GRID_W), :].astype(BF16)
        q2 = q_ref[j * q_rows:(j + 1) * q_rows, :] * (NA_HEAD_DIM ** -0.5 * LOG2E)
        low, q_heads = _split_pair(q2)
        for hh in range(2):
            s_loc = _dot_nt(q_heads[hh], kw) + bias_ref[hh, pattern]
            s_ctx = _dot_nt(q2[:, hh * NA_HEAD_DIM:(hh + 1) * NA_HEAD_DIM].astype(BF16), kcb[hh])
            scores.append((s_loc, s_ctx))
    weights = [_softmax_parts(list(sc)) for sc in scores]
    for j in range(NA_SUB):
        grp = i * NA_SUB + j
        win0 = jnp.clip(NA_ROWS * grp - NA_KH // 2, 0, n_rows - NA_WIN)
        start = pl.multiple_of(win0 * GRID_W, GRID_W)
        vw = v_ref[pl.ds(start, NA_WIN * GRID_W), :].astype(BF16)
        loc, ctx = [], []
        for hh in range(2):
            (e_loc, e_ctx), inv = weights[2 * j + hh]
            loc.append(_dot(e_loc.astype(BF16), vw) * inv)
            ctx.append(_dot(e_ctx.astype(BF16), vcb[hh]) * inv)
        low = lax.broadcasted_iota(jnp.int32, loc[0].shape, 1) < NA_HEAD_DIM
        o_ref[j * q_rows:(j + 1) * q_rows, :] = (jnp.where(low, loc[0], loc[1])
                                                 + jnp.concatenate(ctx, axis=-1)).astype(o_ref.dtype)


def _mixout_kernel(x_ref, u_ref, mod_ref, cb_ref, cc_ref, ch_ref, ccp_ref, chp_ref, ccn_ref, chn_ref,
                   retc_ref, retl_ref, nac_ref, nal_ref, cw_ref, cbias_ref, wm_ref, bm_ref, wb_ref, wo_ref,
                   o_ref, mb_ref, *, tm, n_ctx, seq, dec_seq):
    i = pl.program_id(0)
    is_ctx = i * tm < n_ctx
    ub = u_ref[...]

    z = cc_ref[...] * ch_ref[...]
    z_before = (ccp_ref[...] * chp_ref[...])[HALO - 1:HALO, :]
    z_after = (ccn_ref[...] * chn_ref[...])[0:1, :]
    row = lax.broadcasted_iota(jnp.int32, z.shape, 0)
    last_pos = jnp.where(is_ctx, seq - 1, dec_seq - 1)
    pos = (i * tm - jnp.where(is_ctx, 0, n_ctx) + row) & last_pos
    z_prev = jnp.where(row == 0, z_before, pltpu.roll(z, 1, 0))
    z_prev = jnp.where(pos == 0, 0.0, z_prev)
    z_next = jnp.where(row == tm - 1, z_after, pltpu.roll(z, tm - 1, 0))
    z_next = jnp.where(pos == last_pos, 0.0, z_next)
    y = z_prev * cw_ref[0:1, :] + z * cw_ref[1:2, :] + z_next * cw_ref[2:3, :] + cbias_ref[...]
    conv_out = cb_ref[...] * y

    ret = jnp.where(is_ctx, retc_ref[...], retl_ref[...])
    na = jnp.where(is_ctx, nac_ref[...], nal_ref[...])
    branches = (ret, conv_out.astype(BF16), na)
    for c in range(D_MODEL // FF_CHUNK):
        merged = None
        for b in range(N_BRANCH):
            sl = slice(b * D_MODEL + c * FF_CHUNK, b * D_MODEL + (c + 1) * FF_CHUNK)
            gate = jax.nn.sigmoid(_dot(ub, wm_ref[:, sl]) + bm_ref[:, sl])
            term = gate * _dot(branches[b], wb_ref[b, :, c * FF_CHUNK:(c + 1) * FF_CHUNK])
            merged = term if merged is None else merged + term
        mb_ref[:, c * FF_CHUNK:(c + 1) * FF_CHUNK] = merged.astype(BF16)
    o_ref[...] = x_ref[...] + mod_ref[5:6, :] * _dot(mb_ref[...], wo_ref[...])


def _rope_tables(length, dim):
    t = jnp.arange(length)
    half = dim // 2
    quarter = half // 2
    inv_freq = ROPE_BASE ** (-jnp.arange(quarter, dtype=F32) * 2.0 / half)

    def tables(pos):
        ang = pos.astype(F32)[:, None] * inv_freq[None, :]
        cos, sin = jnp.cos(ang), jnp.sin(ang)
        return jnp.concatenate([cos, cos], axis=-1), jnp.concatenate([-sin, sin], axis=-1)

    cr, sr = tables(t // GRID_W)
    cc, sc = tables(t % GRID_W)
    return jnp.concatenate([cr, cc], axis=-1), jnp.concatenate([sr, sc], axis=-1)


def _decay_tables(decay_logit):
    c = RET_CHUNK
    log_g = jax.nn.log_sigmoid(decay_logit.astype(F32))
    idx = jnp.arange(c, dtype=F32)
    diff = idx[:, None] - idx[None, :]
    out = []
    for d in range(2):
        lg = log_g[d]
        decay_in = jnp.where(diff >= 0, jnp.exp(lg[:, None, None] * jnp.maximum(diff, 0.0)), 0.0)
        xi = jnp.exp(lg[:, None] * (idx + 1.0))
        zeta = jnp.exp(lg[:, None] * (c - 1.0 - idx))
        g_chunk = jnp.exp(lg * c)
        if d == 1:
            decay_in = jnp.swapaxes(decay_in, 1, 2)
            xi = xi[:, ::-1]
            zeta = zeta[:, ::-1]
        full = (c, c)
        out.append(jnp.stack([decay_in,
                              jnp.broadcast_to(xi[:, :, None], (N_RET_HEADS,) + full),
                              jnp.broadcast_to(zeta[:, :, None], (N_RET_HEADS,) + full),
                              jnp.broadcast_to(g_chunk[:, None, None], (N_RET_HEADS,) + full)], axis=1))
    return jnp.stack(out, axis=0)


def _na_bias_rows(rpb):
    lead = GRID_W - NA_KW
    pad = [(0, 0)] * (rpb.ndim - 1) + [(lead, 2 * GRID_W - lead - rpb.shape[-1])]
    return jnp.pad(rpb.astype(F32) * LOG2E, pad, constant_values=NEG_INF)


def _na_row_offsets():
    rr = np.arange(NA_ROWS)
    kr = np.arange(NA_WIN)
    rel_r = np.stack([rr, rr + NA_KH // 2, rr + NA_WIN - NA_ROWS])
    rel_r0 = np.stack([np.zeros_like(rr), rr, np.full_like(rr, NA_WIN - NA_KH)])
    row_ok = (kr[None, None, :] >= rel_r0[:, :, None]) & (kr[None, None, :] < rel_r0[:, :, None] + NA_KH)
    return np.where(row_ok, kr[None, None, :] - rel_r[:, :, None] + (NA_KH - 1), -1)


def kernel(x_prompt, x_sample, c, state_ret, cache_na_k, cache_na_v, c_ctx, norm_g, w_mod, b_mod, ffn_w1, ffn_w2,
           w_in, ret_decay_logit, conv_w, conv_b, na_rpb, w_branch, w_merge, b_merge, w_out, final_g):
    batch, seq, d = x_prompt.shape
    dec_batch, dec_seq, _ = x_sample.shape
    depth = norm_g.shape[0]
    past_len = cache_na_k.shape[3]
    n_ctx = batch * seq
    n_lat = dec_batch * dec_seq
    n_tok = n_ctx + n_lat
    tm = TOKEN_BLOCK
    tf = FFN_BLOCK
    n_rows = dec_seq // GRID_W
    assert d == D_MODEL and 1 + dec_batch <= COND_ROWS
    assert n_ctx % tm == 0 and dec_seq % tm == 0 and tm % seq == 0 and n_ctx % tf == 0 and dec_seq % tf == 0
    assert seq & (seq - 1) == 0 and dec_seq & (dec_seq - 1) == 0 and batch % CTX_SEQS == 0
    assert seq % RET_CHUNK == 0 and dec_seq % (RET_STEP * RET_CHUNK) == 0 and n_ctx % n_lat == 0
    assert n_rows % (NA_ROWS * NA_SUB) == 0 and n_rows >= NA_WIN and D_FF % FF_CHUNK == 0
    n_blocks = n_tok // tm
    ctx_blocks = n_ctx // tm

    x = (x_prompt.reshape(n_ctx, d), x_sample.reshape(n_lat, d))
    cond = jnp.zeros((COND_ROWS, d), F32).at[0].set(c_ctx).at[1:1 + dec_batch].set(c)
    mod = _mod_call(cond, w_mod, b_mod).reshape(depth, COND_ROWS, N_MOD, d)

    w1_b = ffn_w1.astype(BF16)
    w2_b = ffn_w2.astype(BF16)
    w_in_b = w_in.astype(BF16)
    w_merge_b = w_merge.astype(BF16)
    w_branch_b = w_branch.astype(BF16)
    w_out_b = w_out.astype(BF16)
    b_merge3 = b_merge.reshape(depth, 1, N_BRANCH * d)
    conv_b3 = conv_b.reshape(depth, 1, BRANCH_W)
    rope_cos, rope_sin = _rope_tables(dec_seq, RET_DK)
    rope_cos = jnp.concatenate([jnp.ones((tm, RET_DK), F32), rope_cos], axis=0)
    rope_sin = jnp.concatenate([jnp.zeros((tm, RET_DK), F32), rope_sin], axis=0)
    n_chunks_lat = dec_seq // RET_CHUNK
    n_chunks_ctx = seq // RET_CHUNK
    na_steps = n_rows // (NA_ROWS * NA_SUB)
    proj_w = N_SPLIT * BRANCH_W

    def mod_spec(l, rows=tm):
        def index(i):
            row = i * rows
            return (l, jnp.where(row < n_ctx, 0, 1 + (row - n_ctx) // dec_seq), 0, 0)
        return pl.BlockSpec((None, None, N_MOD, d), index)

    def g_spec(l):
        return pl.BlockSpec((None, 3, d), lambda i: (l, 0, 0))

    x_spec = pl.BlockSpec((tm, d), lambda i: (i, 0))
    x_shape = jax.ShapeDtypeStruct((n_tok, d), F32)
    hbm_spec = pl.BlockSpec(memory_space=pl.ANY)

    def ffn(xv, l, s, final=False):
        ctx_steps = n_ctx // tf
        rows_spec = pl.BlockSpec((tf, d), lambda i: (i, 0))
        ctx_spec = pl.BlockSpec((tf, d), lambda i: (jnp.minimum(i, ctx_steps - 1), 0))
        lat_spec = pl.BlockSpec((tf, d), lambda i: (jnp.maximum(i - ctx_steps, 0), 0))
        split_in = isinstance(xv, tuple)
        x_in = list(xv) if split_in else [xv]
        extra_in = [final_g.reshape(1, d)] if final else []
        if final:
            out_specs = [ctx_spec, lat_spec]
            out_shape = [jax.ShapeDtypeStruct((n_ctx, d), F32), jax.ShapeDtypeStruct((n_lat, d), F32)]
        else:
            out_specs, out_shape = rows_spec, x_shape
        return pl.pallas_call(
            functools.partial(_ffn_kernel, gi=2 * s, mi=6 * s, ctx_steps=ctx_steps, split_in=split_in, final=final),
            grid=(n_tok // tf,),
            in_specs=([ctx_spec, lat_spec] if split_in else [rows_spec])
            + [mod_spec(l, tf), g_spec(l),
               _resident((None, None, d, 2 * D_FF), lambda i: (l, s, 0, 0)),
               _resident((None, None, D_FF, d), lambda i: (l, s, 0, 0))]
            + [pl.BlockSpec((1, d), lambda i: (0, 0))] * len(extra_in),
            out_specs=out_specs,
            out_shape=out_shape,
            scratch_shapes=[pltpu.VMEM((tf, d), BF16), pltpu.VMEM((tf, D_FF), BF16)],
            compiler_params=_params(1),
            name="ffn",
        )(*x_in, mod, norm_g, w1_b, w2_b, *extra_in)

    def mixin(xv, l):
        def table_block(i):
            row = i * tm
            return (jnp.where(row < n_ctx, 0, 1 + ((row - n_ctx) % dec_seq) // tm), 0)

        tab = pl.BlockSpec((tm, RET_DK), table_block)
        return pl.pallas_call(
            _mixin_kernel,
            grid=(n_blocks,),
            in_specs=[x_spec, mod_spec(l), g_spec(l), _resident((None, d, proj_w), lambda i: (l, 0, 0)), tab, tab],
            out_specs=[pl.BlockSpec((tm, proj_w), lambda i: (i, 0)), x_spec],
            out_shape=[jax.ShapeDtypeStruct((n_tok, proj_w), F32), jax.ShapeDtypeStruct((n_tok, d), BF16)],
            compiler_params=_params(1),
            name="mixer_in",
        )(xv, mod, norm_g, w_in_b, rope_cos, rope_sin)

    def stacked(shape, l):
        zeros = (0,) * len(shape)
        if l == 0:
            return pl.BlockSpec((CTX_SEQS, depth) + shape, lambda b: (b, 0) + zeros)
        return pl.BlockSpec((CTX_SEQS, None) + shape, lambda b: (b, l) + zeros)

    ctx_rows_per_step = CTX_SEQS * seq

    def ret_ctx(proj, dec, states, l):
        col = lambda k: pl.BlockSpec((ctx_rows_per_step, BRANCH_W), lambda b: (b, k))
        state_shape = (2, N_RET_HEADS, RET_DK, RET_DV)
        carried = () if l == 0 else (states,)
        return pl.pallas_call(
            functools.partial(_ret_ctx_kernel, n_chunks=n_chunks_ctx, first_layer=l == 0),
            grid=(batch // CTX_SEQS,),
            in_specs=[col(0), col(1), col(2), col(3),
                      _resident((None, 2, N_RET_HEADS, 4, RET_CHUNK, RET_CHUNK), lambda b: (l, 0, 0, 0, 0, 0))]
            + [hbm_spec] * len(carried),
            out_specs=[pl.BlockSpec((ctx_rows_per_step, BRANCH_W), lambda b: (b, 0)), stacked(state_shape, l)],
            out_shape=[jax.ShapeDtypeStruct((n_ctx, BRANCH_W), BF16),
                       jax.ShapeDtypeStruct((batch, depth) + state_shape, F32)],
            input_output_aliases={5: 1} if carried else {},
            compiler_params=_params(1),
            name="retention_ctx",
        )(proj, proj, proj, proj, dec, *carried)

    def ret_lat(proj, dec, l):
        proj4 = proj.reshape(n_tok // n_lat, dec_batch, dec_seq, proj_w)
        lat = n_ctx // n_lat

        n_steps = n_chunks_lat // RET_STEP
        step_rows = RET_STEP * RET_CHUNK

        def block_of(p, si):
            return jnp.where(p == 0, si, n_steps - 1 - si)

        col = lambda k: pl.BlockSpec((None, dec_batch, step_rows, BRANCH_W),
                                     lambda p, si: (lat, 0, block_of(p, si), k))
        return pl.pallas_call(
            functools.partial(_ret_lat_kernel, n_steps=n_steps, n_seqs=dec_batch),
            grid=(2, n_steps),
            in_specs=[col(0), col(1), col(2), col(3),
                      pl.BlockSpec((dec_batch, None, None, N_RET_HEADS, RET_DK, RET_DV),
                                   lambda p, si: (0, l, p, 0, 0, 0)),
                      pl.BlockSpec((None, None, N_RET_HEADS, 4, RET_CHUNK, RET_CHUNK),
                                   lambda p, si: (l, p, 0, 0, 0, 0))],
            out_specs=pl.BlockSpec((dec_batch, step_rows, BRANCH_W),
                                   lambda p, si: (0, n_steps - 1 - jnp.where(p == 0, 0, si), 0)),
            out_shape=jax.ShapeDtypeStruct((dec_batch, dec_seq, BRANCH_W), BF16),
            scratch_shapes=[pltpu.VMEM((dec_batch, N_RET_HEADS, RET_DK, RET_DV), F32),
                            pltpu.VMEM((dec_batch, dec_seq, BRANCH_W), F32)],
            compiler_params=_params(2),
            name="retention_lat",
        )(proj4, proj4, proj4, proj4, state_ret, dec).reshape(n_lat, BRANCH_W)

    def na_ctx(proj, ck, cv, l):
        col = lambda k: pl.BlockSpec((ctx_rows_per_step, BRANCH_W), lambda b: (b, k))
        slab = (N_NA_HEADS, seq, NA_HEAD_DIM)
        cache_shape = jax.ShapeDtypeStruct((batch, depth) + slab, F32)
        carried = () if l == 0 else (ck, cv)
        return pl.pallas_call(
            functools.partial(_ctx_attn_kernel, first_layer=l == 0),
            grid=(batch // CTX_SEQS,),
            in_specs=[col(7), col(8), col(9)] + [hbm_spec] * len(carried),
            out_specs=[pl.BlockSpec((ctx_rows_per_step, BRANCH_W), lambda b: (b, 0)), stacked(slab, l),
                       stacked(slab, l)],
            out_shape=[jax.ShapeDtypeStruct((n_ctx, BRANCH_W), BF16), cache_shape, cache_shape],
            input_output_aliases={3: 1, 4: 2} if carried else {},
            compiler_params=_params(1),
            name="attention_ctx",
        )(proj, proj, proj, *carried)

    def na_lat(proj, rpb_rows, l):
        q_rows = NA_SUB * NA_ROWS * GRID_W
        pairs = N_NA_HEADS // 2
        q_base = n_ctx // q_rows
        kv_base = n_ctx // dec_seq
        cache_spec = pl.BlockSpec((None, None, 2, past_len, NA_HEAD_DIM), lambda hp, b, i: (b, l, hp, 0, 0))
        return pl.pallas_call(
            functools.partial(_na_kernel, n_rows=n_rows),
            grid=(pairs, dec_batch, na_steps),
            in_specs=[pl.BlockSpec((q_rows, LANES), lambda hp, b, i: (q_base + b * na_steps + i, 7 * pairs + hp)),
                      pl.BlockSpec((dec_seq, LANES), lambda hp, b, i: (kv_base + b, 8 * pairs + hp)),
                      pl.BlockSpec((dec_seq, LANES), lambda hp, b, i: (kv_base + b, 9 * pairs + hp)),
                      cache_spec, cache_spec,
                      pl.BlockSpec((None, 2, 2 * NA_KH - 1, 2 * GRID_W), lambda hp, b, i: (l, hp, 0, 0))],
            out_specs=pl.BlockSpec((q_rows, LANES), lambda hp, b, i: (b * na_steps + i, hp)),
            out_shape=jax.ShapeDtypeStruct((n_lat, BRANCH_W), BF16),
            scratch_shapes=[pltpu.VMEM((2, 3, NA_ROWS * GRID_W, NA_WIN * GRID_W), F32)],
            compiler_params=_params(3),
            name="attention_lat",
        )(proj, proj, proj, cache_na_k, cache_na_v, rpb_rows)

    def mixout(xv, ub, proj, ret_c, ret_l, na_c, na_l, l):
        col = lambda k: pl.BlockSpec((tm, BRANCH_W), lambda i: (i, k))
        before = lambda k: pl.BlockSpec((HALO, BRANCH_W), lambda i: (jnp.maximum(i * (tm // HALO) - 1, 0), k))
        after = lambda k: pl.BlockSpec((HALO, BRANCH_W),
                                       lambda i: (jnp.minimum((i + 1) * (tm // HALO), n_tok // HALO - 1), k))
        ctx_rows = pl.BlockSpec((tm, BRANCH_W), lambda i: (jnp.minimum(i, ctx_blocks - 1), 0))
        lat_rows = pl.BlockSpec((tm, BRANCH_W), lambda i: (jnp.maximum(i - ctx_blocks, 0), 0))
        return pl.pallas_call(
            functools.partial(_mixout_kernel, tm=tm, n_ctx=n_ctx, seq=seq, dec_seq=dec_seq),
            grid=(n_blocks,),
            in_specs=[x_spec, x_spec, mod_spec(l), col(4), col(5), col(6), before(5), before(6), after(5),
                      after(6), ctx_rows, lat_rows, ctx_rows, lat_rows,
                      pl.BlockSpec((None, CONV_K, BRANCH_W), lambda i: (l, 0, 0)),
                      pl.BlockSpec((None, 1, BRANCH_W), lambda i: (l, 0, 0)),
                      _resident((None, d, N_BRANCH * d), lambda i: (l, 0, 0)),
                      pl.BlockSpec((None, 1, N_BRANCH * d), lambda i: (l, 0, 0)),
                      _resident((None, N_BRANCH, BRANCH_W, d), lambda i: (l, 0, 0, 0)),
                      _resident((None, d, d), lambda i: (l, 0, 0))],
            out_specs=x_spec,
            out_shape=x_shape,
            scratch_shapes=[pltpu.VMEM((tm, d), BF16)],
            compiler_params=_params(1),
            name="mixer_out",
        )(xv, ub, mod, proj, proj, proj, proj, proj, proj, proj, ret_c, ret_l, na_c, na_l, conv_w, conv_b3,
          w_merge_b, b_merge3, w_branch_b, w_out_b)

    dec_all = jax.vmap(_decay_tables)(ret_decay_logit)
    rpb_rows = _na_bias_rows(na_rpb)
    states = cache_k = cache_v = None
    for l in range(depth):
        x = ffn(x, l, 0)
        proj, ub = mixin(x, l)
        ret_c, states = ret_ctx(proj, dec_all, states, l)
        ret_l = ret_lat(proj, dec_all, l)
        na_c, cache_k, cache_v = na_ctx(proj, cache_k, cache_v, l)
        na_l = na_lat(proj, rpb_rows, l)
        x = mixout(x, ub, proj, ret_c, ret_l, na_c, na_l, l)
        x = ffn(x, l, 1, final=l == depth - 1)

    y_prompt, y_sample = x
    return (y_prompt.reshape(batch, seq, d), y_sample.reshape(dec_batch, dec_seq, d), states, cache_k, cache_v)
```

```python
import functools

import jax
import jax.numpy as jnp
import numpy as np
from jax import lax
from jax.experimental import pallas as pl
from jax.experimental.pallas import tpu as pltpu

D_MODEL = 1024
GRID_W = 64
N_RET_HEADS = 4
RET_DK = 128
RET_DV = 128
RET_CHUNK = 128
CONV_K = 3
N_NA_HEADS = 8
NA_HEAD_DIM = 64
NA_KH = 8
NA_KW = 16
BRANCH_W = 512
N_BRANCH = 3
N_SPLIT = 10
D_FF = 2816
N_MOD = 9
ROPE_BASE = 10000.0
EPS = 1e-6
NEG_INF = -1e30
LOG2E = 1.4426950408889634

BF16 = jnp.bfloat16
F32 = jnp.float32

VMEM_LIMIT_BYTES = 56 * 1024 * 1024
LANES = 128
COND_ROWS = 8
MOD_COLS = 2304
TOKEN_BLOCK = 512
FFN_BLOCK = 1024
FF_CHUNK = 256
RET_STEP = 4
CTX_SEQS = 2
NA_ROWS = 4
NA_WIN = NA_ROWS + NA_KH - 1
NA_SUB = 8
HALO = 8


def _params(n_axes):
    return pltpu.CompilerParams(dimension_semantics=("arbitrary",) * n_axes,
                                vmem_limit_bytes=VMEM_LIMIT_BYTES)


def _resident(block_shape, index_map):
    return pl.BlockSpec(block_shape, index_map, pipeline_mode=pl.Buffered(1))


def _dot(a, b):
    return jnp.dot(a, b, preferred_element_type=F32)


def _dot_nt(a, b):
    return lax.dot_general(a, b, (((1,), (1,)), ((), ())), preferred_element_type=F32)


def _norm_mod(x, g_row, scale_row, shift_row):
    ms = jnp.mean(x * x, axis=-1, keepdims=True)
    y = x * lax.rsqrt(ms + EPS)
    return (y * g_row) * (1.0 + scale_row) + shift_row


def _mod_kernel(cond_ref, w_ref, b_ref, o_ref):
    a = jax.nn.silu(cond_ref[...]).astype(BF16)
    o_ref[...] = _dot(a, w_ref[...].astype(BF16)) + b_ref[...]


def _mod_call(cond, w_mod, b_mod):
    depth, d, n = w_mod.shape
    tn = MOD_COLS
    return pl.pallas_call(
        _mod_kernel,
        grid=(depth, n // tn),
        in_specs=[pl.BlockSpec((COND_ROWS, d), lambda l, j: (0, 0)),
                  pl.BlockSpec((None, d, tn), lambda l, j: (l, 0, j)),
                  pl.BlockSpec((None, 1, tn), lambda l, j: (l, 0, j))],
        out_specs=pl.BlockSpec((None, COND_ROWS, tn), lambda l, j: (l, 0, j)),
        out_shape=jax.ShapeDtypeStruct((depth, COND_ROWS, n), F32),
        compiler_params=_params(2),
        name="mod_vectors",
    )(cond, w_mod, b_mod.reshape(depth, 1, n))


def _ffn_kernel(*refs, gi, mi, ctx_steps, split_in, final):
    refs = list(refs)
    x_refs = [refs.pop(0) for _ in range(2 if split_in else 1)]
    mod_ref, g_ref, w1_ref, w2_ref = (refs.pop(0) for _ in range(4))
    fg_ref = refs.pop(0) if final else None
    o_refs = [refs.pop(0) for _ in range(2 if final else 1)]
    hb_ref, t_ref = refs
    is_ctx = pl.program_id(0) < ctx_steps

    def rows():
        return jnp.where(is_ctx, x_refs[0][...], x_refs[1][...]) if split_in else x_refs[0][...]

    h = _norm_mod(rows(), g_ref[gi:gi + 1, :], mod_ref[mi + 1:mi + 2, :], mod_ref[mi:mi + 1, :])
    hb_ref[...] = h.astype(BF16)
    for j in range(D_FF // FF_CHUNK):
        hb = hb_ref[...]
        a = _dot(hb, w1_ref[:, j * FF_CHUNK:(j + 1) * FF_CHUNK])
        b = _dot(hb, w1_ref[:, D_FF + j * FF_CHUNK:D_FF + (j + 1) * FF_CHUNK])
        t_ref[:, j * FF_CHUNK:(j + 1) * FF_CHUNK] = (jax.nn.silu(a) * b).astype(BF16)
    x_new = rows() + (0.5 * mod_ref[mi + 2:mi + 3, :]) * _dot(t_ref[...], w2_ref[...])
    if not final:
        o_refs[0][...] = x_new
        return
    ms = jnp.mean(x_new * x_new, axis=-1, keepdims=True)
    y = (x_new * lax.rsqrt(ms + EPS)) * fg_ref[...]

    @pl.when(is_ctx)
    def _():
        o_refs[0][...] = y

    @pl.when(jnp.logical_not(is_ctx))
    def _():
        o_refs[1][...] = y


def _mixin_kernel(x_ref, mod_ref, g_ref, w_ref, cos_ref, sin_ref, o_ref, u_ref):
    u = _norm_mod(x_ref[...], g_ref[1:2, :], mod_ref[4:5, :], mod_ref[3:4, :]).astype(BF16)
    u_ref[...] = u
    cos = cos_ref[...]
    sin = sin_ref[...]
    lane = lax.broadcasted_iota(jnp.int32, cos.shape, 1)
    first = (lane % (RET_DK // 2)) < (RET_DK // 4)
    for k in range(N_SPLIT):
        sl = slice(k * BRANCH_W, (k + 1) * BRANCH_W)
        r = _dot(u, w_ref[:, sl])
        if k < 2:
            heads = []
            for h in range(N_RET_HEADS):
                xh = r[:, h * RET_DK:(h + 1) * RET_DK]
                if k == 0:
                    xh = xh * (RET_DK ** -0.5)
                swapped = jnp.where(first, pltpu.roll(xh, RET_DK - RET_DK // 4, 1), pltpu.roll(xh, RET_DK // 4, 1))
                heads.append(xh * cos + swapped * sin)
            r = jnp.concatenate(heads, axis=-1)
        o_ref[:, sl] = r


def _layer_slab(refs, first_layer):
    if not first_layer:
        return refs
    for r in refs:
        if r.shape[1] > 1:
            r[:, 1:] = jnp.zeros((r.shape[0], r.shape[1] - 1) + r.shape[2:], r.dtype)
    return [r.at[:, 0] for r in refs]


def _decayed_keys_t(k_tiles, zeta_tiles):
    kz = jnp.concatenate([k * z for k, z in zip(k_tiles, zeta_tiles)], axis=-1)
    kzt = kz.T.astype(BF16)
    return [kzt[i * RET_DK:(i + 1) * RET_DK] for i in range(len(k_tiles))]


def _ret_level(items):
    stage = []
    for q, k, kzt, v, s, dm, xi, gc in items:
        qb = q.astype(BF16)
        vb = v.astype(BF16)
        stage.append((vb, s, dm, xi, gc, _dot_nt(qb, k.astype(BF16)), _dot(qb, s.astype(BF16)), _dot(kzt, vb)))
    out = []
    for vb, s, dm, xi, gc, qk, qs, kv in stage:
        s_new = s * gc + kv
        out.append((_dot((qk * dm).astype(BF16), vb) + qs * xi, s_new))
    return out


def _head_norm_gate(os, gs):
    mus = [jnp.mean(o, axis=-1, keepdims=True) for o in os]
    ds = [o - mu for o, mu in zip(os, mus)]
    vs = [jnp.mean(jnp.square(d), axis=-1, keepdims=True) for d in ds]
    return [(d * lax.rsqrt(v + EPS)) * jax.nn.silu(g) for d, v, g in zip(ds, vs, gs)]


def _ret_ctx_kernel(q_ref, k_ref, v_ref, g_ref, dec_ref, *rest, n_chunks, first_layer):
    st_ref = _layer_slab(rest[-1:], first_layer)[0]
    o_ref = rest[-2]
    c = RET_CHUNK
    heads = [slice(h * RET_DK, (h + 1) * RET_DK) for h in range(N_RET_HEADS)]
    rows = {(s, ci): slice((s * n_chunks + ci) * c, (s * n_chunks + ci + 1) * c)
            for s in range(CTX_SEQS) for ci in range(n_chunks)}
    chains = [(s, d, h) for s in range(CTX_SEQS) for d in range(2) for h in range(N_RET_HEADS)]
    state = {ch: jnp.zeros((RET_DK, RET_DV), F32) for ch in chains}
    o_tot = {}
    for t in range(n_chunks):
        at = {(s, d, h): rows[s, t if d == 0 else n_chunks - 1 - t] for s, d, h in chains}
        kzt = _decayed_keys_t([k_ref[at[ch], heads[ch[2]]] for ch in chains], [dec_ref[d, h, 2] for _, d, h in chains])
        items = [(q_ref[at[ch], heads[ch[2]]], k_ref[at[ch], heads[ch[2]]], kzt_ch, v_ref[at[ch], heads[ch[2]]],
                  state[ch], dec_ref[ch[1], ch[2], 0], dec_ref[ch[1], ch[2], 1], dec_ref[ch[1], ch[2], 3])
                 for ch, kzt_ch in zip(chains, kzt)]
        for ch, (o, s_new) in zip(chains, _ret_level(items)):
            state[ch] = s_new
            key = (at[ch].start, ch[2])
            o_tot[key] = o if key not in o_tot else o_tot[key] + o
    for s, d, h in chains:
        st_ref[s, d, h] = state[s, d, h]
    tiles = [(r, h) for r in rows.values() for h in range(N_RET_HEADS)]
    normed = _head_norm_gate([o_tot[r.start, h] for r, h in tiles], [g_ref[r, heads[h]] for r, h in tiles])
    for (r, h), y in zip(tiles, normed):
        o_ref[r, heads[h]] = y.astype(o_ref.dtype)


def _ret_lat_kernel(q_ref, k_ref, v_ref, g_ref, s0_ref, dec_ref, o_ref, s_scr, of_scr, *, n_steps, n_seqs):
    p = pl.program_id(0)
    si = pl.program_id(1)

    @pl.when(si == 0)
    def _():
        s_scr[...] = s0_ref[...]

    block = jnp.where(p == 0, si, n_steps - 1 - si)
    heads = [slice(h * RET_DK, (h + 1) * RET_DK) for h in range(N_RET_HEADS)]
    tiles = [(b, h) for b in range(n_seqs) for h in range(N_RET_HEADS)]
    state = {bh: s_scr[bh] for bh in tiles}
    done = []
    for t in range(RET_STEP):
        first_row = pl.multiple_of(jnp.where(p == 0, t, RET_STEP - 1 - t) * RET_CHUNK, RET_CHUNK)
        rows = pl.ds(first_row, RET_CHUNK)
        kzt = _decayed_keys_t([k_ref[b, rows, heads[h]] for b, h in tiles], [dec_ref[h, 2] for _, h in tiles])
        items = [(q_ref[b, rows, heads[h]], k_ref[b, rows, heads[h]], kzt_bh, v_ref[b, rows, heads[h]], state[b, h],
                  dec_ref[h, 0], dec_ref[h, 1], dec_ref[h, 3]) for (b, h), kzt_bh in zip(tiles, kzt)]
        os = []
        for bh, (o, s_new) in zip(tiles, _ret_level(items)):
            state[bh] = s_new
            os.append(o)
        done.append((first_row, rows, os))
    for bh in tiles:
        s_scr[bh] = state[bh]

    def parked(first_row):
        return pl.ds(pl.multiple_of(block * (RET_STEP * RET_CHUNK) + first_row, RET_CHUNK), RET_CHUNK)

    @pl.when(p == 0)
    def _():
        for first_row, _, os in done:
            for (b, h), o in zip(tiles, os):
                of_scr[b, parked(first_row), heads[h]] = o

    @pl.when(p == 1)
    def _():
        for first_row, rows, os in done:
            tots = [of_scr[b, parked(first_row), heads[h]] + o for (b, h), o in zip(tiles, os)]
            normed = _head_norm_gate(tots, [g_ref[b, rows, heads[h]] for b, h in tiles])
            for (b, h), y in zip(tiles, normed):
                o_ref[b, rows, heads[h]] = y.astype(o_ref.dtype)


def _softmax_parts(parts):
    m = functools.reduce(jnp.maximum, [jnp.max(s, axis=-1, keepdims=True) for s in parts])
    es = [jnp.exp2(s - m) for s in parts]
    den = functools.reduce(lambda a, b: a + b, [jnp.sum(e, axis=-1, keepdims=True) for e in es])
    return es, 1.0 / den


def _split_pair(q2):
    low = lax.broadcasted_iota(jnp.int32, q2.shape, 1) < NA_HEAD_DIM
    return low, (jnp.where(low, q2, 0.0).astype(BF16), jnp.where(low, 0.0, q2).astype(BF16))


def _ctx_attn_kernel(q_ref, k_ref, v_ref, *rest, first_layer):
    o_ref = rest[-3]
    ck_ref, cv_ref = _layer_slab(rest[-2:], first_layer)
    seq = q_ref.shape[0] // CTX_SEQS
    tiles = [(s, hp, slice(s * seq, (s + 1) * seq), slice(hp * LANES, (hp + 1) * LANES))
             for s in range(CTX_SEQS) for hp in range(N_NA_HEADS // 2)]
    scores = []
    for s, hp, rows, sl in tiles:
        k2 = k_ref[rows, sl]
        v2 = v_ref[rows, sl]
        for hh in range(2):
            ck_ref[s, 2 * hp + hh] = k2[:, hh * NA_HEAD_DIM:(hh + 1) * NA_HEAD_DIM]
            cv_ref[s, 2 * hp + hh] = v2[:, hh * NA_HEAD_DIM:(hh + 1) * NA_HEAD_DIM]
        kb = k2.astype(BF16)
        low, q_heads = _split_pair(q_ref[rows, sl] * (NA_HEAD_DIM ** -0.5 * LOG2E))
        scores += [_dot_nt(qh, kb) for qh in q_heads]
    weights = [_softmax_parts([sc]) for sc in scores]
    for t, (s, hp, rows, sl) in enumerate(tiles):
        vb = v_ref[rows, sl].astype(BF16)
        outs = []
        for hh in range(2):
            (e,), inv = weights[2 * t + hh]
            outs.append(_dot(e.astype(BF16), vb) * inv)
        o_ref[rows, sl] = jnp.where(low, outs[0], outs[1]).astype(o_ref.dtype)


def _na_kernel(q_ref, k_ref, v_ref, kc_ref, vc_ref, rpb_ref, o_ref, bias_ref, *, n_rows):
    i = pl.program_id(2)
    q_rows = NA_ROWS * GRID_W
    n_groups = n_rows // NA_ROWS

    @pl.when((pl.program_id(1) == 0) & (i == 0))
    def _():
        w = GRID_W
        masked = jnp.full((w, w), NEG_INF, F32)
        qc = lax.broadcasted_iota(jnp.int32, (w, w), 0)
        kc = lax.broadcasted_iota(jnp.int32, (w, w), 1)
        c0 = jnp.clip(qc - NA_KW // 2, 0, w - NA_KW)
        col_ok = (kc >= c0) & (kc < c0 + NA_KW)
        offsets = _na_row_offsets()
        for hh in range(2):
            tiles = {}
            for dr in sorted({int(o) for o in offsets.ravel() if o >= 0}):
                rows = jnp.broadcast_to(rpb_ref[hh, dr:dr + 1, :], (w, 2 * w))
                skewed = pltpu.roll(rows, w + 1, 1, stride=1, stride_axis=0)
                tiles[dr] = jnp.where(col_ok, skewed[:, :w], NEG_INF)
            for p in range(3):
                for r in range(NA_ROWS):
                    row = [masked if offsets[p, r, k] < 0 else tiles[int(offsets[p, r, k])] for k in range(NA_WIN)]
                    bias_ref[hh, p, r * w:(r + 1) * w, :] = jnp.concatenate(row, axis=-1)

    kcb = [kc_ref[hh].astype(BF16) for hh in range(2)]
    vcb = [vc_ref[hh].astype(BF16) for hh in range(2)]
    scores = []
    for j in range(NA_SUB):
        grp = i * NA_SUB + j
        pattern = jnp.where(grp == 0, 0, jnp.where(grp == n_groups - 1, 2, 1))
        win0 = jnp.clip(NA_ROWS * grp - NA_KH // 2, 0, n_rows - NA_WIN)
        start = pl.multiple_of(win0 * GRID_W, GRID_W)
        kw = k_ref[pl.ds(start, NA_WIN * GRID_W), :].astype(BF16)
        q2 = q_ref[j * q_rows:(j + 1) * q_rows, :] * (NA_HEAD_DIM ** -0.5 * LOG2E)
        low, q_heads = _split_pair(q2)
        for hh in range(2):
            s_loc = _dot_nt(q_heads[hh], kw) + bias_ref[hh, pattern]
            s_ctx = _dot_nt(q2[:, hh * NA_HEAD_DIM:(hh + 1) * NA_HEAD_DIM].astype(BF16), kcb[hh])
            scores.append((s_loc, s_ctx))
    weights = [_softmax_parts(list(sc)) for sc in scores]
    for j in range(NA_SUB):
        grp = i * NA_SUB + j
        win0 = jnp.clip(NA_ROWS * grp - NA_KH // 2, 0, n_rows - NA_WIN)
        start = pl.multiple_of(win0 * GRID_W, GRID_W)
        vw = v_ref[pl.ds(start, NA_WIN * GRID_W), :].astype(BF16)
        loc, ctx = [], []
        for hh in range(2):
            (e_loc, e_ctx), inv = weights[2 * j + hh]
            loc.append(_dot(e_loc.astype(BF16), vw) * inv)
            ctx.append(_dot(e_ctx.astype(BF16), vcb[hh]) * inv)
        low = lax.broadcasted_iota(jnp.int32, loc[0].shape, 1) < NA_HEAD_DIM
        o_ref[j * q_rows:(j + 1) * q_rows, :] = (jnp.where(low, loc[0], loc[1])
                                                 + jnp.concatenate(ctx, axis=-1)).astype(o_ref.dtype)


def _mixout_kernel(x_ref, u_ref, mod_ref, cb_ref, cc_ref, ch_ref, ccp_ref, chp_ref, ccn_ref, chn_ref,
                   retc_ref, retl_ref, nac_ref, nal_ref, cw_ref, cbias_ref, wm_ref, bm_ref, wb_ref, wo_ref,
                   o_ref, mb_ref, *, tm, n_ctx, seq, dec_seq):
    i = pl.program_id(0)
    is_ctx = i * tm < n_ctx
    ub = u_ref[...]

    z = cc_ref[...] * ch_ref[...]
    z_before = (ccp_ref[...] * chp_ref[...])[HALO - 1:HALO, :]
    z_after = (ccn_ref[...] * chn_ref[...])[0:1, :]
    row = lax.broadcasted_iota(jnp.int32, z.shape, 0)
    last_pos = jnp.where(is_ctx, seq - 1, dec_seq - 1)
    pos = (i * tm - jnp.where(is_ctx, 0, n_ctx) + row) & last_pos
    z_prev = jnp.where(row == 0, z_before, pltpu.roll(z, 1, 0))
    z_prev = jnp.where(pos == 0, 0.0, z_prev)
    z_next = jnp.where(row == tm - 1, z_after, pltpu.roll(z, tm - 1, 0))
    z_next = jnp.where(pos == last_pos, 0.0, z_next)
    y = z_prev * cw_ref[0:1, :] + z * cw_ref[1:2, :] + z_next * cw_ref[2:3, :] + cbias_ref[...]
    conv_out = cb_ref[...] * y

    ret = jnp.where(is_ctx, retc_ref[...], retl_ref[...])
    na = jnp.where(is_ctx, nac_ref[...], nal_ref[...])
    branches = (ret, conv_out.astype(BF16), na)
    for c in range(D_MODEL // FF_CHUNK):
        merged = None
        for b in range(N_BRANCH):
            sl = slice(b * D_MODEL + c * FF_CHUNK, b * D_MODEL + (c + 1) * FF_CHUNK)
            gate = jax.nn.sigmoid(_dot(ub, wm_ref[:, sl]) + bm_ref[:, sl])
            term = gate * _dot(branches[b], wb_ref[b, :, c * FF_CHUNK:(c + 1) * FF_CHUNK])
            merged = term if merged is None else merged + term
        mb_ref[:, c * FF_CHUNK:(c + 1) * FF_CHUNK] = merged.astype(BF16)
    o_ref[...] = x_ref[...] + mod_ref[5:6, :] * _dot(mb_ref[...], wo_ref[...])


def _rope_tables(length, dim):
    t = jnp.arange(length)
    half = dim // 2
    quarter = half // 2
    inv_freq = ROPE_BASE ** (-jnp.arange(quarter, dtype=F32) * 2.0 / half)

    def tables(pos):
        ang = pos.astype(F32)[:, None] * inv_freq[None, :]
        cos, sin = jnp.cos(ang), jnp.sin(ang)
        return jnp.concatenate([cos, cos], axis=-1), jnp.concatenate([-sin, sin], axis=-1)

    cr, sr = tables(t // GRID_W)
    cc, sc = tables(t % GRID_W)
    return jnp.concatenate([cr, cc], axis=-1), jnp.concatenate([sr, sc], axis=-1)


def _decay_tables(decay_logit):
    c = RET_CHUNK
    log_g = jax.nn.log_sigmoid(decay_logit.astype(F32))
    idx = jnp.arange(c, dtype=F32)
    diff = idx[:, None] - idx[None, :]
    out = []
    for d in range(2):
        lg = log_g[d]
        decay_in = jnp.where(diff >= 0, jnp.exp(lg[:, None, None] * jnp.maximum(diff, 0.0)), 0.0)
        xi = jnp.exp(lg[:, None] * (idx + 1.0))
        zeta = jnp.exp(lg[:, None] * (c - 1.0 - idx))
        g_chunk = jnp.exp(lg * c)
        if d == 1:
            decay_in = jnp.swapaxes(decay_in, 1, 2)
            xi = xi[:, ::-1]
            zeta = zeta[:, ::-1]
        full = (c, c)
        out.append(jnp.stack([decay_in,
                              jnp.broadcast_to(xi[:, :, None], (N_RET_HEADS,) + full),
                              jnp.broadcast_to(zeta[:, :, None], (N_RET_HEADS,) + full),
                              jnp.broadcast_to(g_chunk[:, None, None], (N_RET_HEADS,) + full)], axis=1))
    return jnp.stack(out, axis=0)


def _na_bias_rows(rpb):
    lead = GRID_W - NA_KW
    pad = [(0, 0)] * (rpb.ndim - 1) + [(lead, 2 * GRID_W - lead - rpb.shape[-1])]
    return jnp.pad(rpb.astype(F32) * LOG2E, pad, constant_values=NEG_INF)


def _na_row_offsets():
    rr = np.arange(NA_ROWS)
    kr = np.arange(NA_WIN)
    rel_r = np.stack([rr, rr + NA_KH // 2, rr + NA_WIN - NA_ROWS])
    rel_r0 = np.stack([np.zeros_like(rr), rr, np.full_like(rr, NA_WIN - NA_KH)])
    row_ok = (kr[None, None, :] >= rel_r0[:, :, None]) & (kr[None, None, :] < rel_r0[:, :, None] + NA_KH)
    return np.where(row_ok, kr[None, None, :] - rel_r[:, :, None] + (NA_KH - 1), -1)


def kernel(x_prompt, x_sample, c, state_ret, cache_na_k, cache_na_v, c_ctx, norm_g, w_mod, b_mod, ffn_w1, ffn_w2,
           w_in, ret_decay_logit, conv_w, conv_b, na_rpb, w_branch, w_merge, b_merge, w_out, final_g):
    batch, seq, d = x_prompt.shape
    dec_batch, dec_seq, _ = x_sample.shape
    depth = norm_g.shape[0]
    past_len = cache_na_k.shape[3]
    n_ctx = batch * seq
    n_lat = dec_batch * dec_seq
    n_tok = n_ctx + n_lat
    tm = TOKEN_BLOCK
    tf = FFN_BLOCK
    n_rows = dec_seq // GRID_W
    assert d == D_MODEL and 1 + dec_batch <= COND_ROWS
    assert n_ctx % tm == 0 and dec_seq % tm == 0 and tm % seq == 0 and n_ctx % tf == 0 and dec_seq % tf == 0
    assert seq & (seq - 1) == 0 and dec_seq & (dec_seq - 1) == 0 and batch % CTX_SEQS == 0
    assert seq % RET_CHUNK == 0 and dec_seq % (RET_STEP * RET_CHUNK) == 0 and n_ctx % n_lat == 0
    assert n_rows % (NA_ROWS * NA_SUB) == 0 and n_rows >= NA_WIN and D_FF % FF_CHUNK == 0
    n_blocks = n_tok // tm
    ctx_blocks = n_ctx // tm

    x = (x_prompt.reshape(n_ctx, d), x_sample.reshape(n_lat, d))
    cond = jnp.zeros((COND_ROWS, d), F32).at[0].set(c_ctx).at[1:1 + dec_batch].set(c)
    mod = _mod_call(cond, w_mod, b_mod).reshape(depth, COND_ROWS, N_MOD, d)

    w1_b = ffn_w1.astype(BF16)
    w2_b = ffn_w2.astype(BF16)
    w_in_b = w_in.astype(BF16)
    w_merge_b = w_merge.astype(BF16)
    w_branch_b = w_branch.astype(BF16)
    w_out_b = w_out.astype(BF16)
    b_merge3 = b_merge.reshape(depth, 1, N_BRANCH * d)
    conv_b3 = conv_b.reshape(depth, 1, BRANCH_W)
    rope_cos, rope_sin = _rope_tables(dec_seq, RET_DK)
    rope_cos = jnp.concatenate([jnp.ones((tm, RET_DK), F32), rope_cos], axis=0)
    rope_sin = jnp.concatenate([jnp.zeros((tm, RET_DK), F32), rope_sin], axis=0)
    n_chunks_lat = dec_seq // RET_CHUNK
    n_chunks_ctx = seq // RET_CHUNK
    na_steps = n_rows // (NA_ROWS * NA_SUB)
    proj_w = N_SPLIT * BRANCH_W

    def mod_spec(l, rows=tm):
        def index(i):
            row = i * rows
            return (l, jnp.where(row < n_ctx, 0, 1 + (row - n_ctx) // dec_seq), 0, 0)
        return pl.BlockSpec((None, None, N_MOD, d), index)

    def g_spec(l):
        return pl.BlockSpec((None, 3, d), lambda i: (l, 0, 0))

    x_spec = pl.BlockSpec((tm, d), lambda i: (i, 0))
    x_shape = jax.ShapeDtypeStruct((n_tok, d), F32)
    hbm_spec = pl.BlockSpec(memory_space=pl.ANY)

    def ffn(xv, l, s, final=False):
        ctx_steps = n_ctx // tf
        rows_spec = pl.BlockSpec((tf, d), lambda i: (i, 0))
        ctx_spec = pl.BlockSpec((tf, d), lambda i: (jnp.minimum(i, ctx_steps - 1), 0))
        lat_spec = pl.BlockSpec((tf, d), lambda i: (jnp.maximum(i - ctx_steps, 0), 0))
        split_in = isinstance(xv, tuple)
        x_in = list(xv) if split_in else [xv]
        extra_in = [final_g.reshape(1, d)] if final else []
        if final:
            out_specs = [ctx_spec, lat_spec]
            out_shape = [jax.ShapeDtypeStruct((n_ctx, d), F32), jax.ShapeDtypeStruct((n_lat, d), F32)]
        else:
            out_specs, out_shape = rows_spec, x_shape
        return pl.pallas_call(
            functools.partial(_ffn_kernel, gi=2 * s, mi=6 * s, ctx_steps=ctx_steps, split_in=split_in, final=final),
            grid=(n_tok // tf,),
            in_specs=([ctx_spec, lat_spec] if split_in else [rows_spec])
            + [mod_spec(l, tf), g_spec(l),
               _resident((None, None, d, 2 * D_FF), lambda i: (l, s, 0, 0)),
               _resident((None, None, D_FF, d), lambda i: (l, s, 0, 0))]
            + [pl.BlockSpec((1, d), lambda i: (0, 0))] * len(extra_in),
            out_specs=out_specs,
            out_shape=out_shape,
            scratch_shapes=[pltpu.VMEM((tf, d), BF16), pltpu.VMEM((tf, D_FF), BF16)],
            compiler_params=_params(1),
            name="ffn",
        )(*x_in, mod, norm_g, w1_b, w2_b, *extra_in)

    def mixin(xv, l):
        def table_block(i):
            row = i * tm
            return (jnp.where(row < n_ctx, 0, 1 + ((row - n_ctx) % dec_seq) // tm), 0)

        tab = pl.BlockSpec((tm, RET_DK), table_block)
        return pl.pallas_call(
            _mixin_kernel,
            grid=(n_blocks,),
            in_specs=[x_spec, mod_spec(l), g_spec(l), _resident((None, d, proj_w), lambda i: (l, 0, 0)), tab, tab],
            out_specs=[pl.BlockSpec((tm, proj_w), lambda i: (i, 0)), x_spec],
            out_shape=[jax.ShapeDtypeStruct((n_tok, proj_w), F32), jax.ShapeDtypeStruct((n_tok, d), BF16)],
            compiler_params=_params(1),
            name="mixer_in",
        )(xv, mod, norm_g, w_in_b, rope_cos, rope_sin)

    def stacked(shape, l):
        zeros = (0,) * len(shape)
        if l == 0:
            return pl.BlockSpec((CTX_SEQS, depth) + shape, lambda b: (b, 0) + zeros)
        return pl.BlockSpec((CTX_SEQS, None) + shape, lambda b: (b, l) + zeros)

    ctx_rows_per_step = CTX_SEQS * seq

    def ret_ctx(proj, dec, states, l):
        col = lambda k: pl.BlockSpec((ctx_rows_per_step, BRANCH_W), lambda b: (b, k))
        state_shape = (2, N_RET_HEADS, RET_DK, RET_DV)
        carried = () if l == 0 else (states,)
        return pl.pallas_call(
            functools.partial(_ret_ctx_kernel, n_chunks=n_chunks_ctx, first_layer=l == 0),
            grid=(batch // CTX_SEQS,),
            in_specs=[col(0), col(1), col(2), col(3),
                      _resident((None, 2, N_RET_HEADS, 4, RET_CHUNK, RET_CHUNK), lambda b: (l, 0, 0, 0, 0, 0))]
            + [hbm_spec] * len(carried),
            out_specs=[pl.BlockSpec((ctx_rows_per_step, BRANCH_W), lambda b: (b, 0)), stacked(state_shape, l)],
            out_shape=[jax.ShapeDtypeStruct((n_ctx, BRANCH_W), BF16),
                       jax.ShapeDtypeStruct((batch, depth) + state_shape, F32)],
            input_output_aliases={5: 1} if carried else {},
            compiler_params=_params(1),
            name="retention_ctx",
        )(proj, proj, proj, proj, dec, *carried)

    def ret_lat(proj, dec, l):
        proj4 = proj.reshape(n_tok // n_lat, dec_batch, dec_seq, proj_w)
        lat = n_ctx // n_lat

        n_steps = n_chunks_lat // RET_STEP
        step_rows = RET_STEP * RET_CHUNK

        def block_of(p, si):
            return jnp.where(p == 0, si, n_steps - 1 - si)

        col = lambda k: pl.BlockSpec((None, dec_batch, step_rows, BRANCH_W),
                                     lambda p, si: (lat, 0, block_of(p, si), k))
        return pl.pallas_call(
            functools.partial(_ret_lat_kernel, n_steps=n_steps, n_seqs=dec_batch),
            grid=(2, n_steps),
            in_specs=[col(0), col(1), col(2), col(3),
                      pl.BlockSpec((dec_batch, None, None, N_RET_HEADS, RET_DK, RET_DV),
                                   lambda p, si: (0, l, p, 0, 0, 0)),
                      pl.BlockSpec((None, None, N_RET_HEADS, 4, RET_CHUNK, RET_CHUNK),
                                   lambda p, si: (l, p, 0, 0, 0, 0))],
            out_specs=pl.BlockSpec((dec_batch, step_rows, BRANCH_W),
                                   lambda p, si: (0, n_steps - 1 - jnp.where(p == 0, 0, si), 0)),
            out_shape=jax.ShapeDtypeStruct((dec_batch, dec_seq, BRANCH_W), BF16),
            scratch_shapes=[pltpu.VMEM((dec_batch, N_RET_HEADS, RET_DK, RET_DV), F32),
                            pltpu.VMEM((dec_batch, dec_seq, BRANCH_W), F32)],
            compiler_params=_params(2),
            name="retention_lat",
        )(proj4, proj4, proj4, proj4, state_ret, dec).reshape(n_lat, BRANCH_W)

    def na_ctx(proj, ck, cv, l):
        col = lambda k: pl.BlockSpec((ctx_rows_per_step, BRANCH_W), lambda b: (b, k))
        slab = (N_NA_HEADS, seq, NA_HEAD_DIM)
        cache_shape = jax.ShapeDtypeStruct((batch, depth) + slab, F32)
        carried = () if l == 0 else (ck, cv)
        return pl.pallas_call(
            functools.partial(_ctx_attn_kernel, first_layer=l == 0),
            grid=(batch // CTX_SEQS,),
            in_specs=[col(7), col(8), col(9)] + [hbm_spec] * len(carried),
            out_specs=[pl.BlockSpec((ctx_rows_per_step, BRANCH_W), lambda b: (b, 0)), stacked(slab, l),
                       stacked(slab, l)],
            out_shape=[jax.ShapeDtypeStruct((n_ctx, BRANCH_W), BF16), cache_shape, cache_shape],
            input_output_aliases={3: 1, 4: 2} if carried else {},
            compiler_params=_params(1),
            name="attention_ctx",
        )(proj, proj, proj, *carried)

    def na_lat(proj, rpb_rows, l):
        q_rows = NA_SUB * NA_ROWS * GRID_W
        pairs = N_NA_HEADS // 2
        q_base = n_ctx // q_rows
        kv_base = n_ctx // dec_seq
        cache_spec = pl.BlockSpec((None, None, 2, past_len, NA_HEAD_DIM), lambda hp, b, i: (b, l, hp, 0, 0))
        return pl.pallas_call(
            functools.partial(_na_kernel, n_rows=n_rows),
            grid=(pairs, dec_batch, na_steps),
            in_specs=[pl.BlockSpec((q_rows, LANES), lambda hp, b, i: (q_base + b * na_steps + i, 7 * pairs + hp)),
                      pl.BlockSpec((dec_seq, LANES), lambda hp, b, i: (kv_base + b, 8 * pairs + hp)),
                      pl.BlockSpec((dec_seq, LANES), lambda hp, b, i: (kv_base + b, 9 * pairs + hp)),
                      cache_spec, cache_spec,
                      pl.BlockSpec((None, 2, 2 * NA_KH - 1, 2 * GRID_W), lambda hp, b, i: (l, hp, 0, 0))],
            out_specs=pl.BlockSpec((q_rows, LANES), lambda hp, b, i: (b * na_steps + i, hp)),
            out_shape=jax.ShapeDtypeStruct((n_lat, BRANCH_W), BF16),
            scratch_shapes=[pltpu.VMEM((2, 3, NA_ROWS * GRID_W, NA_WIN * GRID_W), F32)],
            compiler_params=_params(3),
            name="attention_lat",
        )(proj, proj, proj, cache_na_k, cache_na_v, rpb_rows)

    def mixout(xv, ub, proj, ret_c, ret_l, na_c, na_l, l):
        col = lambda k: pl.BlockSpec((tm, BRANCH_W), lambda i: (i, k))
        before = lambda k: pl.BlockSpec((HALO, BRANCH_W), lambda i: (jnp.maximum(i * (tm // HALO) - 1, 0), k))
        after = lambda k: pl.BlockSpec((HALO, BRANCH_W),
                                       lambda i: (jnp.minimum((i + 1) * (tm // HALO), n_tok // HALO - 1), k))
        ctx_rows = pl.BlockSpec((tm, BRANCH_W), lambda i: (jnp.minimum(i, ctx_blocks - 1), 0))
        lat_rows = pl.BlockSpec((tm, BRANCH_W), lambda i: (jnp.maximum(i - ctx_blocks, 0), 0))
        return pl.pallas_call(
            functools.partial(_mixout_kernel, tm=tm, n_ctx=n_ctx, seq=seq, dec_seq=dec_seq),
            grid=(n_blocks,),
            in_specs=[x_spec, x_spec, mod_spec(l), col(4), col(5), col(6), before(5), before(6), after(5),
                      after(6), ctx_rows, lat_rows, ctx_rows, lat_rows,
                      pl.BlockSpec((None, CONV_K, BRANCH_W), lambda i: (l, 0, 0)),
                      pl.BlockSpec((None, 1, BRANCH_W), lambda i: (l, 0, 0)),
                      _resident((None, d, N_BRANCH * d), lambda i: (l, 0, 0)),
                      pl.BlockSpec((None, 1, N_BRANCH * d), lambda i: (l, 0, 0)),
                      _resident((None, N_BRANCH, BRANCH_W, d), lambda i: (l, 0, 0, 0)),
                      _resident((None, d, d), lambda i: (l, 0, 0))],
            out_specs=x_spec,
            out_shape=x_shape,
            scratch_shapes=[pltpu.VMEM((tm, d), BF16)],
            compiler_params=_params(1),
            name="mixer_out",
        )(xv, ub, mod, proj, proj, proj, proj, proj, proj, proj, ret_c, ret_l, na_c, na_l, conv_w, conv_b3,
          w_merge_b, b_merge3, w_branch_b, w_out_b)

    dec_all = jax.vmap(_decay_tables)(ret_decay_logit)
    rpb_rows = _na_bias_rows(na_rpb)
    states = cache_k = cache_v = None
    for l in range(depth):
        x = ffn(x, l, 0)
        proj, ub = mixin(x, l)
        ret_c, states = ret_ctx(proj, dec_all, states, l)
        ret_l = ret_lat(proj, dec_all, l)
        na_c, cache_k, cache_v = na_ctx(proj, cache_k, cache_v, l)
        na_l = na_lat(proj, rpb_rows, l)
        x = mixout(x, ub, proj, ret_c, ret_l, na_c, na_l, l)
        x = ffn(x, l, 1, final=l == depth - 1)

    y_prompt, y_sample = x
    return (y_prompt.reshape(batch, seq, d), y_sample.reshape(dec_batch, dec_seq, d), states, cache_k, cache_v)
```
